```python
import jax, jax.numpy as jnp
from jax import lax
import numpy as np

D_MODEL = 1024
BATCH = 4
SEQ = 8192
DEPTH = 2

GRID_W = 64
MEM_LEN = 256
EPS = 1e-6

NA_HEADS = 8
NA_HEAD_DIM = 64
NA_WIN_H_MAX = 8
NA_WIN_W = 16
NA_WIDTH = NA_HEADS * NA_HEAD_DIM

MLA_HEADS = 8
MLA_Q_RANK = 384
MLA_KV_RANK = 256
MLA_NOPE = 64
MLA_ROPE = 32
MLA_V = 64
MLA_QK = MLA_NOPE + MLA_ROPE
MLA_WIDTH = MLA_HEADS * MLA_V
MLA_Q_BLOCK = 128
ROPE_BASE = 10000.0
ROPE_AXIS_FREQS = MLA_ROPE // 4

MIX_WIDTH = NA_WIDTH + MLA_WIDTH
IN_COLS = 3 * NA_WIDTH + MLA_Q_RANK + MLA_KV_RANK + MLA_ROPE

MEM_HEADS = 4
MEM_HEAD_DIM = 128
MEM_WIDTH = MEM_HEADS * MEM_HEAD_DIM

D_FF = 2816
CONV_W = 3

kernel_name = "hybrid_natten_mla_memory_convffn_encoder"


def _rmsnorm(x, g):
    xf = x.astype(jnp.float32)
    y = xf * lax.rsqrt(jnp.mean(xf * xf, axis=-1, keepdims=True) + EPS)
    return (y * g.astype(jnp.float32)).astype(x.dtype)


def _rope_tables(seq):
    t = jnp.arange(seq)
    row = (t // GRID_W).astype(jnp.float32)
    col = (t % GRID_W).astype(jnp.float32)
    inv = 1.0 / (ROPE_BASE ** (jnp.arange(ROPE_AXIS_FREQS, dtype=jnp.float32) / ROPE_AXIS_FREQS))
    ang = jnp.concatenate([row[:, None] * inv, col[:, None] * inv], axis=-1)
    return jnp.cos(ang), jnp.sin(ang)


def _rope(x, cos, sin):
    half = x.shape[-1] // 2
    x1, x2 = x[..., :half], x[..., half:]
    c = cos[None, :, None, :].astype(x.dtype)
    s = sin[None, :, None, :].astype(x.dtype)
    return jnp.concatenate([x1 * c - x2 * s, x2 * c + x1 * s], axis=-1)


def _neighbourhood_attention(q, k, v, rpb):
    B, S, H, dh = q.shape
    rows = S // GRID_W
    kh = min(NA_WIN_H_MAX, rows)
    kw = NA_WIN_W
    qg = q.reshape(B, rows, GRID_W, H, dh)
    kg = k.reshape(B, rows, GRID_W, H, dh)
    vg = v.reshape(B, rows, GRID_W, H, dh)
    cq = jnp.arange(GRID_W)
    c0 = jnp.clip(cq - kw // 2, 0, GRID_W - kw)
    col_idx = c0[:, None] + jnp.arange(kw)[None, :]
    dc_idx = col_idx - cq[:, None] + (NA_WIN_W - 1)
    r0 = jnp.clip(jnp.arange(rows) - kh // 2, 0, rows - kh)
    scale = dh ** -0.5

    def row_step(r):
        start = r0[r]
        q_r = lax.dynamic_index_in_dim(qg, r, axis=1, keepdims=False)
        k_band = lax.dynamic_slice_in_dim(kg, start, kh, axis=1)
        v_band = lax.dynamic_slice_in_dim(vg, start, kh, axis=1)
        k_win = k_band[:, :, col_idx]
        v_win = v_band[:, :, col_idx]
        s = jnp.einsum('bqhd,bjqwhd->bhqjw', q_r, k_win).astype(jnp.float32) * scale
        dr_idx = start + jnp.arange(kh) - r + (NA_WIN_H_MAX - 1)
        bias = rpb[:, dr_idx[None, :, None], dc_idx[:, None, :]]
        s = s + bias[None].astype(jnp.float32)
        p = jax.nn.softmax(s.reshape(B, H, GRID_W, kh * kw), axis=-1)
        p = p.reshape(B, H, GRID_W, kh, kw).astype(v.dtype)
        return jnp.einsum('bhqjw,bjqwhd->bqhd', p, v_win)

    o = lax.map(row_step, jnp.arange(rows))
    return o.transpose(1, 0, 2, 3, 4).reshape(B, S, H * dh)


def _dense_attention_blocks(q, k, v):
    B, S, H, dq = q.shape
    dv = v.shape[-1]
    nb = S // MLA_Q_BLOCK
    qb = q.reshape(B, nb, MLA_Q_BLOCK, H, dq).transpose(1, 0, 2, 3, 4)
    scale = dq ** -0.5

    def block(qi):
        s = jnp.einsum('bqhd,bkhd->bhqk', qi, k).astype(jnp.float32) * scale
        p = jax.nn.softmax(s, axis=-1).astype(v.dtype)
        return jnp.einsum('bhqk,bkhd->bqhd', p, v)

    o = lax.map(block, qb)
    return o.transpose(1, 0, 2, 3, 4).reshape(B, S, H * dv)


def _hybrid_mixer(h, cos, sin, w_in, na_q_g, na_k_g, na_rpb, q_lat_g, kv_lat_g,
                  w_uq, w_ukv, mla_q_g, mla_k_g, grp_out_g, w_out):
    B, S, _ = h.shape
    z = h @ w_in
    o1 = NA_WIDTH
    o2 = 2 * NA_WIDTH
    o3 = 3 * NA_WIDTH
    o4 = o3 + MLA_Q_RANK
    o5 = o4 + MLA_KV_RANK
    q_na, k_na, v_na, c_q, c_kv, k_rope = jnp.split(z, [o1, o2, o3, o4, o5], axis=-1)

    q_na = _rmsnorm(q_na.reshape(B, S, NA_HEADS, NA_HEAD_DIM), na_q_g)
    k_na = _rmsnorm(k_na.reshape(B, S, NA_HEADS, NA_HEAD_DIM), na_k_g)
    v_na = v_na.reshape(B, S, NA_HEADS, NA_HEAD_DIM)
    out_a = _neighbourhood_attention(q_na, k_na, v_na, na_rpb)

    q = (_rmsnorm(c_q, q_lat_g) @ w_uq).reshape(B, S, MLA_HEADS, MLA_QK)
    q = _rmsnorm(q, mla_q_g)
    kv = (_rmsnorm(c_kv, kv_lat_g) @ w_ukv).reshape(B, S, MLA_HEADS, MLA_NOPE + MLA_V)
    k_nope, v = kv[..., :MLA_NOPE], kv[..., MLA_NOPE:]
    k_r = jnp.broadcast_to(k_rope[:, :, None, :], (B, S, MLA_HEADS, MLA_ROPE))
    k = _rmsnorm(jnp.concatenate([k_nope, k_r], axis=-1), mla_k_g)
    q = jnp.concatenate([q[..., :MLA_NOPE], _rope(q[..., MLA_NOPE:], cos, sin)], axis=-1)
    k = jnp.concatenate([k[..., :MLA_NOPE], _rope(k[..., MLA_NOPE:], cos, sin)], axis=-1)
    out_b = _dense_attention_blocks(q, k, v)

    out_a = _rmsnorm(out_a, grp_out_g[:NA_WIDTH])
    out_b = _rmsnorm(out_b, grp_out_g[NA_WIDTH:])
    return jnp.concatenate([out_a, out_b], axis=-1) @ w_out


def _memory_cross_attention(h, mem_n, w_q, w_kv, q_g, k_g, w_o):
    B, S, _ = h.shape
    M = mem_n.shape[1]
    q = _rmsnorm((h @ w_q).reshape(B, S, MEM_HEADS, MEM_HEAD_DIM), q_g)
    kv = mem_n @ w_kv
    k = _rmsnorm(kv[..., :MEM_WIDTH].reshape(B, M, MEM_HEADS, MEM_HEAD_DIM), k_g)
    v = kv[..., MEM_WIDTH:].reshape(B, M, MEM_HEADS, MEM_HEAD_DIM)
    s = jnp.einsum('bshd,bmhd->bhsm', q, k).astype(jnp.float32) * (MEM_HEAD_DIM ** -0.5)
    p = jax.nn.softmax(s, axis=-1).astype(v.dtype)
    o = jnp.einsum('bhsm,bmhd->bshd', p, v).reshape(B, S, MEM_WIDTH)
    return o @ w_o


def _conv_ffn(h, w_up, conv_w, conv_b, w_down):
    u = h @ w_up
    up = jnp.pad(u, ((0, 0), (1, 1), (0, 0)))
    u = up[:, :-2] * conv_w[0] + up[:, 1:-1] * conv_w[1] + up[:, 2:] * conv_w[2] + conv_b
    gate, val = u[..., :D_FF], u[..., D_FF:]
    return (jax.nn.silu(gate) * val) @ w_down


def _w(k, shape, fan_in):
    return jax.random.normal(k, shape, jnp.float32) * (fan_in ** -0.5)


def _g(k, shape):
    return 1.0 + 0.02 * jax.random.normal(k, shape, jnp.float32)


def setup_inputs(seed: int = 0) -> dict:
    key = jax.random.key(seed)
    ks = iter(jax.random.split(key, 32))
    L = DEPTH
    return {
        "x": jax.random.normal(next(ks), (BATCH, SEQ, D_MODEL), jnp.float32),
        "mem": jax.random.normal(next(ks), (BATCH, MEM_LEN, D_MODEL), jnp.float32),
        "mix_norm_g": _g(next(ks), (L, D_MODEL)),
        "w_in": _w(next(ks), (L, D_MODEL, IN_COLS), D_MODEL),
        "na_q_g": _g(next(ks), (L, NA_HEAD_DIM)),
        "na_k_g": _g(next(ks), (L, NA_HEAD_DIM)),
        "na_rpb": 0.02 * jax.random.normal(next(ks), (L, NA_HEADS, 2 * NA_WIN_H_MAX - 1, 2 * NA_WIN_W - 1), jnp.float32),
        "q_lat_g": _g(next(ks), (L, MLA_Q_RANK)),
        "kv_lat_g": _g(next(ks), (L, MLA_KV_RANK)),
        "w_uq": _w(next(ks), (L, MLA_Q_RANK, MLA_HEADS * MLA_QK), MLA_Q_RANK),
        "w_ukv": _w(next(ks), (L, MLA_KV_RANK, MLA_HEADS * (MLA_NOPE + MLA_V)), MLA_KV_RANK),
        "mla_q_g": _g(next(ks), (L, MLA_QK)),
        "mla_k_g": _g(next(ks), (L, MLA_QK)),
        "grp_out_g": _g(next(ks), (L, MIX_WIDTH)),
        "w_out": _w(next(ks), (L, MIX_WIDTH, D_MODEL), MIX_WIDTH),
        "mem_norm_g": _g(next(ks), (L, D_MODEL)),
        "mem_tok_norm_g": _g(next(ks), (L, D_MODEL)),
        "mem_w_q": _w(next(ks), (L, D_MODEL, MEM_WIDTH), D_MODEL),
        "mem_w_kv": _w(next(ks), (L, D_MODEL, 2 * MEM_WIDTH), D_MODEL),
        "mem_q_g": _g(next(ks), (L, MEM_HEAD_DIM)),
        "mem_k_g": _g(next(ks), (L, MEM_HEAD_DIM)),
        "mem_w_o": _w(next(ks), (L, MEM_WIDTH, D_MODEL), MEM_WIDTH),
        "ffn_norm_g": _g(next(ks), (L, D_MODEL)),
        "ffn_w_up": _w(next(ks), (L, D_MODEL, 2 * D_FF), D_MODEL),
        "ffn_conv_w": _w(next(ks), (L, CONV_W, 2 * D_FF), CONV_W),
        "ffn_conv_b": 0.02 * jax.random.normal(next(ks), (L, 2 * D_FF), jnp.float32),
        "ffn_w_down": _w(next(ks), (L, D_FF, D_MODEL), D_FF),
    }


def reference(x, mem, mix_norm_g, w_in, na_q_g, na_k_g, na_rpb, q_lat_g, kv_lat_g,
              w_uq, w_ukv, mla_q_g, mla_k_g, grp_out_g, w_out, mem_norm_g,
              mem_tok_norm_g, mem_w_q, mem_w_kv, mem_q_g, mem_k_g, mem_w_o,
              ffn_norm_g, ffn_w_up, ffn_conv_w, ffn_conv_b, ffn_w_down):
    cos, sin = _rope_tables(x.shape[1])
    for l in range(DEPTH):
        h = _rmsnorm(x, mix_norm_g[l])
        x = x + _hybrid_mixer(h, cos, sin, w_in[l], na_q_g[l], na_k_g[l], na_rpb[l],
                              q_lat_g[l], kv_lat_g[l], w_uq[l], w_ukv[l], mla_q_g[l],
                              mla_k_g[l], grp_out_g[l], w_out[l])
        h = _rmsnorm(x, mem_norm_g[l])
        mem_n = _rmsnorm(mem, mem_tok_norm_g[l])
        x = x + _memory_cross_attention(h, mem_n, mem_w_q[l], mem_w_kv[l], mem_q_g[l],
                                        mem_k_g[l], mem_w_o[l])
        h = _rmsnorm(x, ffn_norm_g[l])
        x = x + _conv_ffn(h, ffn_w_up[l], ffn_conv_w[l], ffn_conv_b[l], ffn_w_down[l])
    return x
```

```python
import functools

import jax
import jax.numpy as jnp
from jax import lax
from jax.experimental import pallas as pl
from jax.experimental.pallas import tpu as pltpu

F32 = jnp.float32
BF16 = jnp.bfloat16

D_MODEL = 1024
GRID_W = 64
EPS = 1e-6
NA_HEADS = 8
NA_HEAD_DIM = 64
NA_WIN_H = 8
NA_WIN_W = 16
NA_WIDTH = NA_HEADS * NA_HEAD_DIM
MLA_HEADS = 8
MLA_Q_RANK = 384
MLA_KV_RANK = 256
MLA_NOPE = 64
MLA_ROPE = 32
MLA_V = 64
MLA_QK = MLA_NOPE + MLA_ROPE
MLA_WIDTH = MLA_HEADS * MLA_V
ROPE_BASE = 10000.0
MEM_HEADS = 4
MEM_HEAD_DIM = 128
MEM_WIDTH = MEM_HEADS * MEM_HEAD_DIM
D_FF = 2816
LANES = 128
NEG = -1e30

VMEM_LIMIT = 48 * 1024 * 1024


def _params(sem):
    return pltpu.CompilerParams(dimension_semantics=sem, vmem_limit_bytes=VMEM_LIMIT)


def _rms(x, g):
    return x * lax.rsqrt(jnp.mean(x * x, axis=-1, keepdims=True) + EPS) * g


def _dot(a, b):
    return jnp.dot(a, b, preferred_element_type=F32)


def _dot_nt(a, b):
    return lax.dot_general(a, b, (((1,), (1,)), ((), ())), preferred_element_type=F32)


def _mixer_in_kernel(x_ref, gmix_ref, wna_ref, wlat_ref, wuq_ref, wuqs_ref, wuk_ref, wuv_ref,
                     gnaq_ref, gnak_ref, gql_ref, gkvl_ref, gq_ref, gqs_ref, gk_ref, gks_ref,
                     cos_ref, sin_ref,
                     qna_ref, kna_ref, vna_ref, qT_ref, k_ref, vT_ref, *, tk):
    x = x_ref[0]
    tm = x.shape[0]
    hn = _rms(x, gmix_ref[...]).astype(BF16)

    zna = _dot(hn, wna_ref[...])
    lo = lax.broadcasted_iota(jnp.int32, (tm, LANES), 1) < NA_HEAD_DIM

    def seg_norm(z, g_ref, gscale, out_ref):
        for c in range(NA_WIDTH // LANES):
            sl = slice(c * LANES, (c + 1) * LANES)
            blk = z[:, sl]
            sq = blk * blk
            s_lo = jnp.sum(jnp.where(lo, sq, 0.0), axis=-1, keepdims=True)
            s_hi = jnp.sum(jnp.where(lo, 0.0, sq), axis=-1, keepdims=True)
            ms = jnp.where(lo, s_lo, s_hi) * (1.0 / NA_HEAD_DIM)
            out_ref[0, :, sl] = (blk * lax.rsqrt(ms + EPS) * (g_ref[:, sl] * gscale)).astype(BF16)

    seg_norm(zna[:, :NA_WIDTH], gnaq_ref, NA_HEAD_DIM ** -0.5, qna_ref)
    seg_norm(zna[:, NA_WIDTH:2 * NA_WIDTH], gnak_ref, 1.0, kna_ref)
    vna_ref[0] = zna[:, 2 * NA_WIDTH:].astype(BF16)

    zlat = _dot(hn, wlat_ref[...])
    o1 = MLA_Q_RANK
    o2 = o1 + MLA_KV_RANK
    cqn = _rms(zlat[:, :o1], gql_ref[...]).astype(BF16)
    ckvn = _rms(zlat[:, o1:o2], gkvl_ref[...]).astype(BF16)
    kr = zlat[:, o2:o2 + LANES]
    krs = zlat[:, o2 + LANES:o2 + 2 * LANES]
    qpre = _dot(cqn, wuq_ref[...])
    qsw = _dot(cqn, wuqs_ref[...])
    kpre = _dot(ckvn, wuk_ref[...])
    v = _dot(ckvn, wuv_ref[...])
    cosv = cos_ref[...]
    sinv = sin_ref[...]
    gq = gq_ref[...]
    gqs = gqs_ref[...]
    gk = gk_ref[...]
    gks = gks_ref[...]
    qscale = MLA_QK ** -0.5
    for h in range(MLA_HEADS):
        sl = slice(h * LANES, (h + 1) * LANES)
        qb = qpre[:, sl]
        rinv = lax.rsqrt(jnp.sum(qb * qb, axis=-1, keepdims=True) * (1.0 / MLA_QK) + EPS)
        qr = ((qb * gq) * cosv + (qsw[:, sl] * gqs) * sinv) * (rinv * qscale)
        qT_ref[0, h] = qr.T.astype(BF16)
        kb = kpre[:, sl] + kr
        rinv = lax.rsqrt(jnp.sum(kb * kb, axis=-1, keepdims=True) * (1.0 / MLA_QK) + EPS)
        k_ref[0, h] = (((kb * gk) * cosv + (krs * gks) * sinv) * rinv).astype(BF16)
    vT = v.T.astype(BF16)
    for h in range(MLA_HEADS):
        for c in range(tm // tk):
            vT_ref[0, h, c] = vT[h * MLA_V:(h + 1) * MLA_V, c * tk:(c + 1) * tk]


def _mixer_in(x, gmix, wna, wlat, wuq, wuqs, wuk, wuv, gnaq, gnak, gql, gkvl, gq, gqs, gk, gks, cos, sin, *, tm, tk):
    B, S, D = x.shape
    nt = S // tm
    full = lambda a: pl.BlockSpec(a.shape, lambda b, i: (0,) * a.ndim)
    consts = (gmix, wna, wlat, wuq, wuqs, wuk, wuv, gnaq, gnak, gql, gkvl, gq, gqs, gk, gks)
    tok = lambda w: pl.BlockSpec((1, tm, w), lambda b, i: (b, i, 0))
    out_shape = (
        jax.ShapeDtypeStruct((B, S, NA_WIDTH), BF16),
        jax.ShapeDtypeStruct((B, S, NA_WIDTH), BF16),
        jax.ShapeDtypeStruct((B, S, NA_WIDTH), BF16),
        jax.ShapeDtypeStruct((B, MLA_HEADS, LANES, S), BF16),
        jax.ShapeDtypeStruct((B, MLA_HEADS, S, LANES), BF16),
        jax.ShapeDtypeStruct((B, MLA_HEADS, S // tk, MLA_V, tk), BF16),
    )
    out_specs = (
        tok(NA_WIDTH), tok(NA_WIDTH), tok(NA_WIDTH),
        pl.BlockSpec((1, MLA_HEADS, LANES, tm), lambda b, i: (b, 0, 0, i)),
        pl.BlockSpec((1, MLA_HEADS, tm, LANES), lambda b, i: (b, 0, i, 0)),
        pl.BlockSpec((1, MLA_HEADS, tm // tk, MLA_V, tk), lambda b, i: (b, 0, i, 0, 0)),
    )
    return pl.pallas_call(
        functools.partial(_mixer_in_kernel, tk=tk),
        grid=(B, nt),
        in_specs=[tok(D)] + [full(a) for a in consts]
        + [pl.BlockSpec((tm, LANES), lambda b, i: (i, 0))] * 2,
        out_specs=out_specs,
        out_shape=out_shape,
        compiler_params=_params(("parallel", "parallel")),
        name="mixer_in",
    )(x, *consts, cos, sin)


def _na_kernel(q_ref, k0_ref, k1_ref, k2_ref, v0_ref, v1_ref, v2_ref, bias_ref, o_ref, kb_ref, vb_ref,
               *, rows_per_step, n_rows):
    g = pl.program_id(1)
    R = rows_per_step
    blk = R * GRID_W
    band = NA_WIN_H * GRID_W
    for n, (kr, vr) in enumerate(((k0_ref, v0_ref), (k1_ref, v1_ref), (k2_ref, v2_ref))):
        kb_ref[n * blk:(n + 1) * blk, :] = kr[0]
        vb_ref[n * blk:(n + 1) * blk, :] = vr[0]
    lo = lax.broadcasted_iota(jnp.int32, (GRID_W, LANES), 1) < NA_HEAD_DIM

    def row_body(i, carry):
        r = g * R + i
        start = jnp.clip(r - NA_WIN_H // 2, 0, n_rows - NA_WIN_H)
        off = pl.multiple_of((start - (g * R - R)) * GRID_W, GRID_W)
        d7 = start - r + (NA_WIN_H - 1)
        qoff = pl.multiple_of(i * GRID_W, GRID_W)
        qrow = q_ref[0, pl.ds(qoff, GRID_W), :].astype(F32)
        kband = kb_ref[pl.ds(off, band), :]
        vband = vb_ref[pl.ds(off, band), :]
        for hp in range(NA_WIDTH // LANES):
            sl = slice(hp * LANES, (hp + 1) * LANES)
            qp = qrow[:, sl]
            kp = kband[:, sl]
            vp = vband[:, sl]
            outs = []
            for sub in range(2):
                h = 2 * hp + sub
                qm = (jnp.where(lo, qp, 0.0) if sub == 0 else jnp.where(lo, 0.0, qp)).astype(BF16)
                s = _dot_nt(qm, kp)
                bias = jnp.concatenate([bias_ref[h, d7 + 2 * p] for p in range(NA_WIN_H // 2)], axis=-1)
                s = s + bias
                m = jnp.max(s, axis=-1, keepdims=True)
                p_ = jnp.exp(s - m)
                l = jnp.sum(p_, axis=-1, keepdims=True)
                outs.append(_dot(p_.astype(BF16), vp) / l)
            o_ref[0, pl.ds(qoff, GRID_W), sl] = jnp.where(lo, outs[0], outs[1])
        return carry

    lax.fori_loop(0, R, row_body, 0)


def _na_attention(q, k, v, bias, *, rows_per_step):
    B, S, W = q.shape
    n_rows = S // GRID_W
    R = rows_per_step
    ng = n_rows // R
    blk = R * GRID_W
    qspec = pl.BlockSpec((1, blk, W), lambda b, g: (b, g, 0))
    prev = pl.BlockSpec((1, blk, W), lambda b, g: (b, jnp.maximum(g - 1, 0), 0))
    nxt = pl.BlockSpec((1, blk, W), lambda b, g: (b, jnp.minimum(g + 1, ng - 1), 0))
    return pl.pallas_call(
        functools.partial(_na_kernel, rows_per_step=R, n_rows=n_rows),
        grid=(B, ng),
        in_specs=[qspec, prev, qspec, nxt, prev, qspec, nxt,
                  pl.BlockSpec(bias.shape, lambda b, g: (0, 0, 0, 0))],
        out_specs=qspec,
        out_shape=jax.ShapeDtypeStruct((B, S, W), F32),
        scratch_shapes=[pltpu.VMEM((3 * blk, W), BF16), pltpu.VMEM((3 * blk, W), BF16)],
        compiler_params=_params(("parallel", "parallel")),
        name="na_attn",
    )(q, k, k, k, v, v, v, bias)


def _na_bias_table(rpb):
    qc = jnp.arange(GRID_W)
    c0 = jnp.clip(qc - NA_WIN_W // 2, 0, GRID_W - NA_WIN_W)
    inwin = (qc[None, :] >= c0[:, None]) & (qc[None, :] < c0[:, None] + NA_WIN_W)
    dc = jnp.clip(qc[None, :] - qc[:, None] + (NA_WIN_W - 1), 0, 2 * NA_WIN_W - 2)
    m2 = jnp.where(inwin[None, None], rpb[:, :, dc], NEG)
    return jnp.concatenate([m2[:, :-1], m2[:, 1:]], axis=-1).astype(F32)


def _mla_kernel(qT_ref, k_ref, vT_ref, o_ref, *, tk):
    qT = qT_ref[0, 0]
    tq = qT.shape[1]
    nk = k_ref.shape[2] // tk

    def body(c, carry):
        m, l, acc = carry
        off = pl.multiple_of(c * tk, tk)
        s = _dot(k_ref[0, 0, pl.ds(off, tk), :], qT)
        m_new = jnp.maximum(m, jnp.max(s, axis=0, keepdims=True))
        alpha = jnp.exp(m - m_new)
        p = jnp.exp(s - m_new)
        l = alpha * l + jnp.sum(p, axis=0, keepdims=True)
        acc = alpha * acc + _dot(vT_ref[0, 0, c], p.astype(BF16))
        return m_new, l, acc

    init = (jnp.full((1, tq), NEG, F32), jnp.zeros((1, tq), F32), jnp.zeros((MLA_V, tq), F32))
    m, l, acc = lax.fori_loop(0, nk, body, init)
    o_ref[0] = acc / l


def _mla_attention(qT, k, vT, *, tq):
    B, H, _, S = qT.shape
    tk = vT.shape[-1]
    return pl.pallas_call(
        functools.partial(_mla_kernel, tk=tk),
        grid=(B, H, S // tq),
        in_specs=[
            pl.BlockSpec((1, 1, LANES, tq), lambda b, h, i: (b, h, 0, i)),
            pl.BlockSpec((1, 1, S, LANES), lambda b, h, i: (b, h, 0, 0)),
            pl.BlockSpec((1, 1, S // tk, MLA_V, tk), lambda b, h, i: (b, h, 0, 0, 0)),
        ],
        out_specs=pl.BlockSpec((1, MLA_V, tq), lambda b, h, i: (b, h, i)),
        out_shape=jax.ShapeDtypeStruct((B, H * MLA_V, S), F32),
        compiler_params=_params(("parallel", "parallel", "parallel")),
        name="mla_attn",
    )(qT, k, vT)


def _mem_kv_kernel(mem_ref, g_ref, wkv_ref, gk_ref, k_ref, v_ref):
    mn = _rms(mem_ref[0], g_ref[...]).astype(BF16)
    kv = _dot(mn, wkv_ref[...])
    gk = gk_ref[...]
    for h in range(MEM_HEADS):
        sl = slice(h * LANES, (h + 1) * LANES)
        k_ref[0, :, sl] = _rms(kv[:, sl], gk).astype(BF16)
    v_ref[0] = kv[:, MEM_WIDTH:].astype(BF16)


def _mem_kv(mem, g, wkv, gk):
    B, M, D = mem.shape
    full = lambda a: pl.BlockSpec(a.shape, lambda b: (0,) * a.ndim)
    out = jax.ShapeDtypeStruct((B, M, MEM_WIDTH), BF16)
    ospec = pl.BlockSpec((1, M, MEM_WIDTH), lambda b: (b, 0, 0))
    return pl.pallas_call(
        _mem_kv_kernel,
        grid=(B,),
        in_specs=[pl.BlockSpec((1, M, D), lambda b: (b, 0, 0)), full(g), full(wkv), full(gk)],
        out_specs=(ospec, ospec),
        out_shape=(out, out),
        compiler_params=_params(("parallel",)),
        name="mem_kv",
    )(mem, g, wkv, gk)


def _post_mixer_kernel(x_ref, a_ref, bT_ref, ga_ref, gb_ref, wout_ref, gmem_ref, wq_ref, gmq_ref,
                       km_ref, vm_ref, wo_ref, o_ref):
    x = x_ref[0]
    an = _rms(a_ref[0], ga_ref[...]).astype(BF16)
    bn = _rms(bT_ref[0].T, gb_ref[...]).astype(BF16)
    x1 = x + _dot(an, wout_ref[:NA_WIDTH, :]) + _dot(bn, wout_ref[NA_WIDTH:, :])

    hq = _rms(x1, gmem_ref[...]).astype(BF16)
    q = _dot(hq, wq_ref[...])
    gq = gmq_ref[...] * (MEM_HEAD_DIM ** -0.5)
    outs = []
    for h in range(MEM_HEADS):
        sl = slice(h * LANES, (h + 1) * LANES)
        qn = _rms(q[:, sl], gq).astype(BF16)
        s = _dot_nt(qn, km_ref[0, :, sl])
        m = jnp.max(s, axis=-1, keepdims=True)
        p = jnp.exp(s - m)
        l = jnp.sum(p, axis=-1, keepdims=True)
        outs.append((_dot(p.astype(BF16), vm_ref[0, :, sl]) / l).astype(BF16))
    o_ref[0] = x1 + _dot(jnp.concatenate(outs, axis=-1), wo_ref[...])


def _post_mixer(x, a, bT, ga, gb, wout, gmem, wq, gmq, km, vm, wo, *, tm):
    B, S, D = x.shape
    M = km.shape[1]
    full = lambda arr: pl.BlockSpec(arr.shape, lambda b, i: (0,) * arr.ndim)
    tok = lambda w: pl.BlockSpec((1, tm, w), lambda b, i: (b, i, 0))
    memspec = pl.BlockSpec((1, M, MEM_WIDTH), lambda b, i: (b, 0, 0))
    return pl.pallas_call(
        _post_mixer_kernel,
        grid=(B, S // tm),
        in_specs=[tok(D), tok(NA_WIDTH), pl.BlockSpec((1, MLA_WIDTH, tm), lambda b, i: (b, 0, i)),
                  full(ga), full(gb), full(wout), full(gmem), full(wq), full(gmq), memspec, memspec, full(wo)],
        out_specs=tok(D),
        out_shape=jax.ShapeDtypeStruct((B, S, D), F32),
        compiler_params=_params(("parallel", "parallel")),
        name="post_mixer",
    )(x, a, bT, ga, gb, wout, gmem, wq, gmq, km, vm, wo)


HALO = 8


def _ffn_kernel(x_ref, xp_ref, xn_ref, g_ref, wg_ref, wv_ref, cwg_ref, cwv_ref, cbg_ref, cbv_ref, wd_ref,
                o_ref, hn_ref, hh_ref, acc_ref, *, tiles_per_seq):
    i = pl.program_id(0)
    j = pl.program_id(1)
    tm = x_ref.shape[0]
    tn = wg_ref.shape[1]

    @pl.when(j == 0)
    def _():
        g = g_ref[...]
        hn_ref[...] = _rms(x_ref[...], g).astype(BF16)
        hh_ref[:HALO, :] = _rms(xp_ref[...], g).astype(BF16)
        hh_ref[HALO:, :] = _rms(xn_ref[...], g).astype(BF16)
        acc_ref[...] = jnp.zeros_like(acc_ref)

    hn = hn_ref[...]
    hh = hh_ref[...]
    has_prev = (i % tiles_per_seq != 0).astype(F32)
    has_next = ((i + 1) % tiles_per_seq != 0).astype(F32)
    row = lax.broadcasted_iota(jnp.int32, (tm, tn), 0)

    def conv(w_ref, cw_ref, cb_ref):
        w = w_ref[...]
        u = _dot(hn, w)
        uh = _dot(hh, w)
        prev = uh[HALO - 1:HALO, :] * has_prev
        nxt = uh[HALO:HALO + 1, :] * has_next
        um = jnp.where(row == 0, prev, pltpu.roll(u, 1, 0))
        up = jnp.where(row == tm - 1, nxt, pltpu.roll(u, tm - 1, 0))
        return um * cw_ref[0:1, :] + u * cw_ref[1:2, :] + up * cw_ref[2:3, :] + cb_ref[...]

    gate = conv(wg_ref, cwg_ref, cbg_ref)
    val = conv(wv_ref, cwv_ref, cbv_ref)
    act = gate / (1.0 + jnp.exp(-gate)) * val
    acc_ref[...] += _dot(act.astype(BF16), wd_ref[...])

    @pl.when(j == pl.num_programs(1) - 1)
    def _():
        o_ref[...] = x_ref[...] + acc_ref[...]


def _conv_ffn(x, g, wup, cw, cb, wd, *, tm, tn, seq):
    T, D = x.shape
    nj = D_FF // tn
    nhalo = T // HALO
    per = tm // HALO
    return pl.pallas_call(
        functools.partial(_ffn_kernel, tiles_per_seq=seq // tm),
        grid=(T // tm, nj),
        in_specs=[
            pl.BlockSpec((tm, D), lambda i, j: (i, 0)),
            pl.BlockSpec((HALO, D), lambda i, j: (jnp.maximum(i * per - 1, 0), 0)),
            pl.BlockSpec((HALO, D), lambda i, j: (jnp.minimum((i + 1) * per, nhalo - 1), 0)),
            pl.BlockSpec(g.shape, lambda i, j: (0, 0)),
            pl.BlockSpec((D, tn), lambda i, j: (0, j)),
            pl.BlockSpec((D, tn), lambda i, j: (0, nj + j)),
            pl.BlockSpec((3, tn), lambda i, j: (0, j)),
            pl.BlockSpec((3, tn), lambda i, j: (0, nj + j)),
            pl.BlockSpec((1, tn), lambda i, j: (0, j)),
            pl.BlockSpec((1, tn), lambda i, j: (0, nj + j)),
            pl.BlockSpec((tn, D), lambda i, j: (j, 0)),
        ],
        out_specs=pl.BlockSpec((tm, D), lambda i, j: (i, 0)),
        out_shape=jax.ShapeDtypeStruct((T, D), F32),
        scratch_shapes=[pltpu.VMEM((tm, D), BF16), pltpu.VMEM((2 * HALO, D), BF16), pltpu.VMEM((tm, D), F32)],
        compiler_params=_params(("parallel", "arbitrary")),
        name="conv_ffn",
    )(x, x, x, g, wup, wup, cw, cw, cb, cb, wd)


def _head_pad(w, heads, width):
    k = w.shape[0]
    w = w.reshape(k, heads, width)
    return jnp.pad(w, ((0, 0), (0, 0), (0, LANES - width))).reshape(k, heads * LANES)


def _rope_swap(a):
    half = a.shape[-1] // 2
    return jnp.concatenate([a[..., half:], a[..., :half]], axis=-1)


def _rope_sign():
    half = MLA_ROPE // 2
    return jnp.concatenate([-jnp.ones((half,), F32), jnp.ones((half,), F32)])


def _rope_tables_padded(seq):
    t = jnp.arange(seq)
    row = (t // GRID_W).astype(F32)
    col = (t % GRID_W).astype(F32)
    nfreq = MLA_ROPE // 4
    inv = 1.0 / (ROPE_BASE ** (jnp.arange(nfreq, dtype=F32) / nfreq))
    ang = jnp.concatenate([row[:, None] * inv, col[:, None] * inv], axis=-1)
    cos = jnp.concatenate([jnp.cos(ang)] * 2, axis=-1)
    sin = jnp.concatenate([jnp.sin(ang)] * 2, axis=-1)
    pad = LANES - MLA_QK
    cos = jnp.concatenate([jnp.ones((seq, MLA_NOPE), F32), cos, jnp.zeros((seq, pad), F32)], axis=-1)
    sin = jnp.concatenate([jnp.zeros((seq, MLA_NOPE), F32), sin, jnp.zeros((seq, pad), F32)], axis=-1)
    return cos, sin


def _layer_params(l, w_in, na_q_g, na_k_g, na_rpb, q_lat_g, kv_lat_g, w_uq, w_ukv, mla_q_g, mla_k_g):
    o1, o2, o3 = NA_WIDTH, 2 * NA_WIDTH, 3 * NA_WIDTH
    o4 = o3 + MLA_Q_RANK
    o5 = o4 + MLA_KV_RANK
    win = w_in[l]
    sign = _rope_sign()
    w_kr = win[:, o5:]
    pad_rope = lambda w: jnp.pad(w, ((0, 0), (MLA_NOPE, LANES - MLA_QK)))
    wlat = jnp.concatenate([win[:, o3:o5], pad_rope(w_kr), pad_rope(_rope_swap(w_kr) * sign)], axis=-1)

    wq3 = w_uq[l].reshape(MLA_Q_RANK, MLA_HEADS, MLA_QK)
    wq3s = jnp.concatenate([jnp.zeros_like(wq3[..., :MLA_NOPE]), _rope_swap(wq3[..., MLA_NOPE:]) * sign], axis=-1)
    wkv3 = w_ukv[l].reshape(MLA_KV_RANK, MLA_HEADS, MLA_NOPE + MLA_V)

    def gain_pair(g):
        gs = jnp.concatenate([jnp.zeros((MLA_NOPE,), F32), _rope_swap(g[MLA_NOPE:])])
        padl = lambda a: jnp.pad(a, (0, LANES - MLA_QK))[None, :]
        return padl(g), padl(gs)

    gq, gqs = gain_pair(mla_q_g[l])
    gk, gks = gain_pair(mla_k_g[l])
    return dict(
        wna=win[:, :o3].astype(BF16),
        wlat=wlat.astype(BF16),
        wuq=_head_pad(wq3.reshape(MLA_Q_RANK, -1), MLA_HEADS, MLA_QK).astype(BF16),
        wuqs=_head_pad(wq3s.reshape(MLA_Q_RANK, -1), MLA_HEADS, MLA_QK).astype(BF16),
        wuk=_head_pad(wkv3[..., :MLA_NOPE].reshape(MLA_KV_RANK, -1), MLA_HEADS, MLA_NOPE).astype(BF16),
        wuv=wkv3[..., MLA_NOPE:].reshape(MLA_KV_RANK, -1).astype(BF16),
        gnaq=jnp.tile(na_q_g[l], NA_HEADS)[None, :],
        gnak=jnp.tile(na_k_g[l], NA_HEADS)[None, :],
        gql=q_lat_g[l][None, :],
        gkvl=kv_lat_g[l][None, :],
        gq=gq, gqs=gqs, gk=gk, gks=gks,
        bias=_na_bias_table(na_rpb[l]),
    )


def _tiles(S):
    tm = min(512, S)
    return dict(tm=tm, tq=min(512, S), tk=min(256, tm), na_rows=8, ffn_tm=min(1024, S), ffn_tn=256)


def kernel(x, mem, mix_norm_g, w_in, na_q_g, na_k_g, na_rpb, q_lat_g, kv_lat_g, w_uq, w_ukv, mla_q_g, mla_k_g,
           grp_out_g, w_out, mem_norm_g, mem_tok_norm_g, mem_w_q, mem_w_kv, mem_q_g, mem_k_g, mem_w_o,
           ffn_norm_g, ffn_w_up, ffn_conv_w, ffn_conv_b, ffn_w_down):
    B, S, D = x.shape
    depth = w_in.shape[0]
    t = _tiles(S)
    assert D == D_MODEL and S % (t["na_rows"] * GRID_W) == 0 and S % t["tm"] == 0 and S % t["ffn_tm"] == 0
    cos, sin = _rope_tables_padded(S)
    row = lambda a: a[None, :]
    for l in range(depth):
        p = _layer_params(l, w_in, na_q_g, na_k_g, na_rpb, q_lat_g, kv_lat_g, w_uq, w_ukv, mla_q_g, mla_k_g)
        qna, kna, vna, qT, k, vT = _mixer_in(
            x, row(mix_norm_g[l]), p["wna"], p["wlat"], p["wuq"], p["wuqs"], p["wuk"], p["wuv"],
            p["gnaq"], p["gnak"], p["gql"], p["gkvl"], p["gq"], p["gqs"], p["gk"], p["gks"], cos, sin,
            tm=t["tm"], tk=t["tk"])
        out_a = _na_attention(qna, kna, vna, p["bias"], rows_per_step=t["na_rows"])
        out_bT = _mla_attention(qT, k, vT, tq=t["tq"])
        km, vm = _mem_kv(mem, row(mem_tok_norm_g[l]), mem_w_kv[l].astype(BF16), row(mem_k_g[l]))
        x = _post_mixer(
            x, out_a, out_bT, row(grp_out_g[l, :NA_WIDTH]), row(grp_out_g[l, NA_WIDTH:]), w_out[l].astype(BF16),
            row(mem_norm_g[l]), mem_w_q[l].astype(BF16), row(mem_q_g[l]), km, vm, mem_w_o[l].astype(BF16),
            tm=t["tm"])
        x = _conv_ffn(
            x.reshape(B * S, D), row(ffn_norm_g[l]), ffn_w_up[l].astype(BF16), ffn_conv_w[l], row(ffn_conv_b[l]),
            ffn_w_down[l].astype(BF16), tm=t["ffn_tm"], tn=t["ffn_tn"], seq=S).reshape(B, S, D)
    return x
```

```python
import functools

import jax
import jax.numpy as jnp
from jax import lax
from jax.experimental import pallas as pl
from jax.experimental.pallas import tpu as pltpu

F32 = jnp.float32
BF16 = jnp.bfloat16

D_MODEL = 1024
GRID_W = 64
EPS = 1e-6
NA_HEADS = 8
NA_HEAD_DIM = 64
NA_WIN_H = 8
NA_WIN_W = 16
NA_WIDTH = NA_HEADS * NA_HEAD_DIM
MLA_HEADS = 8
MLA_Q_RANK = 384
MLA_KV_RANK = 256
MLA_NOPE = 64
MLA_ROPE = 32
MLA_V = 64
MLA_QK = MLA_NOPE + MLA_ROPE
MLA_WIDTH = MLA_HEADS * MLA_V
ROPE_BASE = 10000.0
MEM_HEADS = 4
MEM_HEAD_DIM = 128
MEM_WIDTH = MEM_HEADS * MEM_HEAD_DIM
D_FF = 2816
LANES = 128
NEG = -1e30
LOG2E = 1.4426950408889634
MLA_AUG = MLA_QK
SAFE_BOUND = 50.0

VMEM_LIMIT = 48 * 1024 * 1024


def _params(sem):
    return pltpu.CompilerParams(dimension_semantics=sem, vmem_limit_bytes=VMEM_LIMIT)


def _rms(x, g):
    return x * lax.rsqrt(jnp.mean(x * x, axis=-1, keepdims=True) + EPS) * g


def _dot(a, b):
    return jnp.dot(a, b, preferred_element_type=F32)


def _dot_nt(a, b):
    return lax.dot_general(a, b, (((1,), (1,)), ((), ())), preferred_element_type=F32)


def _mixer_in_kernel(x_ref, gmix_ref, wna_ref, wlat_ref, wuq_ref, wuqs_ref, wuk_ref, wuv_ref,
                     gnaq_ref, gnak_ref, gql_ref, gkvl_ref, gq_ref, gqs_ref, gk_ref, gks_ref,
                     cos_ref, sin_ref,
                     qna_ref, kna_ref, vna_ref, qT_ref, k_ref, vT_ref, qb_ref, *, tk):
    x = x_ref[0]
    tm = x.shape[0]
    hn = _rms(x, gmix_ref[...]).astype(BF16)

    zna = _dot(hn, wna_ref[...])
    lo = lax.broadcasted_iota(jnp.int32, (tm, LANES), 1) < NA_HEAD_DIM

    def seg_norm(z, g_ref, gscale, out_ref):
        for c in range(NA_WIDTH // LANES):
            sl = slice(c * LANES, (c + 1) * LANES)
            blk = z[:, sl]
            sq = blk * blk
            s_lo = jnp.sum(jnp.where(lo, sq, 0.0), axis=-1, keepdims=True)
            s_hi = jnp.sum(jnp.where(lo, 0.0, sq), axis=-1, keepdims=True)
            ms = jnp.where(lo, s_lo, s_hi) * (1.0 / NA_HEAD_DIM)
            out_ref[0, :, sl] = (blk * lax.rsqrt(ms + EPS) * (g_ref[:, sl] * gscale)).astype(BF16)

    seg_norm(zna[:, :NA_WIDTH], gnaq_ref, NA_HEAD_DIM ** -0.5, qna_ref)
    seg_norm(zna[:, NA_WIDTH:2 * NA_WIDTH], gnak_ref, 1.0, kna_ref)
    vna_ref[0] = zna[:, 2 * NA_WIDTH:].astype(BF16)

    zlat = _dot(hn, wlat_ref[...])
    o1 = MLA_Q_RANK
    o2 = o1 + MLA_KV_RANK
    cqn = _rms(zlat[:, :o1], gql_ref[...]).astype(BF16)
    ckvn = _rms(zlat[:, o1:o2], gkvl_ref[...]).astype(BF16)
    kr = zlat[:, o2:o2 + LANES]
    krs = zlat[:, o2 + LANES:o2 + 2 * LANES]
    qpre = _dot(cqn, wuq_ref[...])
    qsw = _dot(cqn, wuqs_ref[...])
    kpre = _dot(ckvn, wuk_ref[...])
    v = _dot(ckvn, wuv_ref[...])
    cosv = cos_ref[...]
    sinv = sin_ref[...]
    gq = gq_ref[...]
    gqs = gqs_ref[...]
    gk = gk_ref[...]
    gks = gks_ref[...]
    qscale = MLA_QK ** -0.5 * LOG2E
    aug = lax.broadcasted_iota(jnp.int32, (tm, LANES), 1) == MLA_AUG
    kmax = (MLA_QK ** 0.5) * jnp.max(jnp.abs(gk), axis=-1, keepdims=True)
    tile_bounds = []
    for h in range(MLA_HEADS):
        sl = slice(h * LANES, (h + 1) * LANES)
        qb = qpre[:, sl]
        rinv = lax.rsqrt(jnp.sum(qb * qb, axis=-1, keepdims=True) * (1.0 / MLA_QK) + EPS)
        qr = ((qb * gq) * cosv + (qsw[:, sl] * gqs) * sinv) * (rinv * qscale)
        bound = jnp.sqrt(jnp.sum(qr * qr, axis=-1, keepdims=True)) * kmax
        qT_ref[0, h] = jnp.where(aug, -bound, qr).T.astype(BF16)
        tile_bounds.append(jnp.broadcast_to(jnp.max(bound, axis=0, keepdims=True), (1, LANES)))
        kb = kpre[:, sl] + kr
        rinv = lax.rsqrt(jnp.sum(kb * kb, axis=-1, keepdims=True) * (1.0 / MLA_QK) + EPS)
        kk = ((kb * gk) * cosv + (krs * gks) * sinv) * rinv
        k_ref[0, h] = jnp.where(aug, 1.0, kk).astype(BF16)
    qb_ref[0, 0] = jnp.concatenate(tile_bounds, axis=0)
    vT = v.T.astype(BF16)
    for h in range(MLA_HEADS):
        for c in range(tm // tk):
            vT_ref[0, h, c] = vT[h * MLA_V:(h + 1) * MLA_V, c * tk:(c + 1) * tk]


def _mixer_in(x, gmix, wna, wlat, wuq, wuqs, wuk, wuv, gnaq, gnak, gql, gkvl, gq, gqs, gk, gks, cos, sin, *, tm, tk):
    B, S, D = x.shape
    nt = S // tm
    full = lambda a: pl.BlockSpec(a.shape, lambda b, i: (0,) * a.ndim)
    consts = (gmix, wna, wlat, wuq, wuqs, wuk, wuv, gnaq, gnak, gql, gkvl, gq, gqs, gk, gks)
    tok = lambda w: pl.BlockSpec((1, tm, w), lambda b, i: (b, i, 0))
    out_shape = (
        jax.ShapeDtypeStruct((B, S, NA_WIDTH), BF16),
        jax.ShapeDtypeStruct((B, S, NA_WIDTH), BF16),
        jax.ShapeDtypeStruct((B, S, NA_WIDTH), BF16),
        jax.ShapeDtypeStruct((B, MLA_HEADS, LANES, S), BF16),
        jax.ShapeDtypeStruct((B, MLA_HEADS, S, LANES), BF16),
        jax.ShapeDtypeStruct((B, MLA_HEADS, S // tk, MLA_V, tk), BF16),
        jax.ShapeDtypeStruct((B, nt, MLA_HEADS, LANES), F32),
    )
    out_specs = (
        tok(NA_WIDTH), tok(NA_WIDTH), tok(NA_WIDTH),
        pl.BlockSpec((1, MLA_HEADS, LANES, tm), lambda b, i: (b, 0, 0, i)),
        pl.BlockSpec((1, MLA_HEADS, tm, LANES), lambda b, i: (b, 0, i, 0)),
        pl.BlockSpec((1, MLA_HEADS, tm // tk, MLA_V, tk), lambda b, i: (b, 0, i, 0, 0)),
        pl.BlockSpec((1, 1, MLA_HEADS, LANES), lambda b, i: (b, i, 0, 0)),
    )
    return pl.pallas_call(
        functools.partial(_mixer_in_kernel, tk=tk),
        grid=(B, nt),
        in_specs=[tok(D)] + [full(a) for a in consts]
        + [pl.BlockSpec((tm, LANES), lambda b, i: (i, 0))] * 2,
        out_specs=out_specs,
        out_shape=out_shape,
        compiler_params=_params(("parallel", "parallel")),
        name="mixer_in",
    )(x, *consts, cos, sin)


def _na_kernel(q_ref, k0_ref, k1_ref, k2_ref, v0_ref, v1_ref, v2_ref, bias_ref, o_ref, kb_ref, vb_ref,
               *, rows_per_step, rows_per_trip, n_rows):
    g = pl.program_id(1)
    R = rows_per_step
    blk = R * GRID_W
    band = NA_WIN_H * GRID_W
    for n, (kr, vr) in enumerate(((k0_ref, v0_ref), (k1_ref, v1_ref), (k2_ref, v2_ref))):
        kb_ref[n * blk:(n + 1) * blk, :] = kr[0]
        vb_ref[n * blk:(n + 1) * blk, :] = vr[0]
    lo = lax.broadcasted_iota(jnp.int32, (GRID_W, LANES), 1) < NA_HEAD_DIM

    def trip(t, carry):
        chains = []
        for u in range(rows_per_trip):
            i = t * rows_per_trip + u
            r = g * R + i
            start = jnp.clip(r - NA_WIN_H // 2, 0, n_rows - NA_WIN_H)
            off = pl.multiple_of((start - (g * R - R)) * GRID_W, GRID_W)
            d7 = start - r + (NA_WIN_H - 1)
            qoff = pl.multiple_of(i * GRID_W, GRID_W)
            chains += [(off, d7, qoff, hp) for hp in range(NA_WIDTH // LANES)]

        def scores(off, d7, qoff, hp):
            sl = slice(hp * LANES, (hp + 1) * LANES)
            qp = q_ref[0, pl.ds(qoff, GRID_W), sl].astype(F32)
            q2 = jnp.concatenate([jnp.where(lo, qp, 0.0), jnp.where(lo, 0.0, qp)], axis=0).astype(BF16)
            return _dot_nt(q2, kb_ref[pl.ds(off, band), sl])

        s2 = scores(*chains[0])
        for n, (off, d7, qoff, hp) in enumerate(chains):
            s_next = scores(*chains[n + 1]) if n + 1 < len(chains) else None
            sl = slice(hp * LANES, (hp + 1) * LANES)
            bias = jnp.concatenate(
                [jnp.concatenate([bias_ref[2 * hp + sub, d7 + 2 * p] for p in range(NA_WIN_H // 2)], axis=-1)
                 for sub in range(2)], axis=0)
            s = s2 + bias
            m = jnp.max(s, axis=-1, keepdims=True)
            p_ = jnp.exp(s - m)
            l = jnp.sum(p_, axis=-1, keepdims=True)
            o2 = _dot(p_.astype(BF16), vb_ref[pl.ds(off, band), sl]) / l
            o_ref[0, pl.ds(qoff, GRID_W), sl] = jnp.where(lo, o2[:GRID_W], o2[GRID_W:])
            s2 = s_next
        return carry

    lax.fori_loop(0, R // rows_per_trip, trip, 0)


def _na_attention(q, k, v, bias, *, rows_per_step, rows_per_trip):
    B, S, W = q.shape
    n_rows = S // GRID_W
    R = rows_per_step
    ng = n_rows // R
    blk = R * GRID_W
    qspec = pl.BlockSpec((1, blk, W), lambda b, g: (b, g, 0))
    prev = pl.BlockSpec((1, blk, W), lambda b, g: (b, jnp.maximum(g - 1, 0), 0))
    nxt = pl.BlockSpec((1, blk, W), lambda b, g: (b, jnp.minimum(g + 1, ng - 1), 0))
    return pl.pallas_call(
        functools.partial(_na_kernel, rows_per_step=R, rows_per_trip=rows_per_trip, n_rows=n_rows),
        grid=(B, ng),
        in_specs=[qspec, prev, qspec, nxt, prev, qspec, nxt,
                  pl.BlockSpec(bias.shape, lambda b, g: (0, 0, 0, 0))],
        out_specs=qspec,
        out_shape=jax.ShapeDtypeStruct((B, S, W), F32),
        scratch_shapes=[pltpu.VMEM((3 * blk, W), BF16), pltpu.VMEM((3 * blk, W), BF16)],
        compiler_params=_params(("parallel", "parallel")),
        name="na_attn",
    )(q, k, k, k, v, v, v, bias)


def _na_bias_table(rpb):
    qc = jnp.arange(GRID_W)
    c0 = jnp.clip(qc - NA_WIN_W // 2, 0, GRID_W - NA_WIN_W)
    inwin = (qc[None, :] >= c0[:, None]) & (qc[None, :] < c0[:, None] + NA_WIN_W)
    dc = jnp.clip(qc[None, :] - qc[:, None] + (NA_WIN_W - 1), 0, 2 * NA_WIN_W - 2)
    m2 = jnp.where(inwin[None, None], rpb[:, :, dc], NEG)
    return jnp.concatenate([m2[:, :-1], m2[:, 1:]], axis=-1).astype(F32)


def _mla_kernel(bound_ref, qT_ref, k_ref, vT_ref, o_ref, *, tk, unroll):
    qa = qT_ref[0, 0]
    tq = qa.shape[1]
    nk = k_ref.shape[2] // tk
    tile = (pl.program_id(0) * pl.num_programs(1) + pl.program_id(1)) * pl.num_programs(2) + pl.program_id(2)
    safe = bound_ref[tile] <= SAFE_BOUND

    def scores(c):
        off = pl.multiple_of(c * tk, tk)
        return _dot(k_ref[0, 0, pl.ds(off, tk), :], qa)

    @pl.when(safe)
    def _():
        def body(t, carry):
            l8, acc = carry
            c0 = t * unroll
            s = scores(c0)
            for u in range(unroll):
                s_next = scores(c0 + u + 1) if u + 1 < unroll else None
                p = jnp.exp2(s)
                l8 = l8 + jnp.sum(p.reshape(tk // 8, 8, tq), axis=0)
                acc = acc + _dot(vT_ref[0, 0, c0 + u], p.astype(BF16))
                s = s_next
            return l8, acc

        init = (jnp.zeros((8, tq), F32), jnp.zeros((MLA_V, tq), F32))
        l8, acc = lax.fori_loop(0, nk // unroll, body, init)
        o_ref[0] = acc / jnp.sum(l8, axis=0, keepdims=True)

    @pl.when(jnp.logical_not(safe))
    def _():
        def body(c, carry):
            m, l, acc = carry
            s = scores(c)
            m_new = jnp.maximum(m, jnp.max(s, axis=0, keepdims=True))
            alpha = jnp.exp2(m - m_new)
            p = jnp.exp2(s - m_new)
            l = alpha * l + jnp.sum(p, axis=0, keepdims=True)
            acc = alpha * acc + _dot(vT_ref[0, 0, c], p.astype(BF16))
            return m_new, l, acc

        init = (jnp.full((1, tq), NEG, F32), jnp.zeros((1, tq), F32), jnp.zeros((MLA_V, tq), F32))
        m, l, acc = lax.fori_loop(0, nk, body, init)
        o_ref[0] = acc / l


def _mla_attention(tile_bound, qT, k, vT, *, unroll):
    B, H, _, S = qT.shape
    tq = S // tile_bound.shape[2]
    tk = vT.shape[-1]
    return pl.pallas_call(
        functools.partial(_mla_kernel, tk=tk, unroll=unroll),
        grid=(B, H, S // tq),
        in_specs=[
            pl.BlockSpec(memory_space=pltpu.SMEM),
            pl.BlockSpec((1, 1, LANES, tq), lambda b, h, i: (b, h, 0, i)),
            pl.BlockSpec((1, 1, S, LANES), lambda b, h, i: (b, h, 0, 0)),
            pl.BlockSpec((1, 1, S // tk, MLA_V, tk), lambda b, h, i: (b, h, 0, 0, 0)),
        ],
        out_specs=pl.BlockSpec((1, MLA_V, tq), lambda b, h, i: (b, h, i)),
        out_shape=jax.ShapeDtypeStruct((B, H * MLA_V, S), F32),
        compiler_params=_params(("parallel", "parallel", "parallel")),
        name="mla_attn",
    )(tile_bound.reshape(-1), qT, k, vT)


def _mem_kv_kernel(mem_ref, g_ref, wkv_ref, gk_ref, k_ref, v_ref):
    mn = _rms(mem_ref[0], g_ref[...]).astype(BF16)
    kv = _dot(mn, wkv_ref[...])
    gk = gk_ref[...]
    for h in range(MEM_HEADS):
        sl = slice(h * LANES, (h + 1) * LANES)
        k_ref[0, :, sl] = _rms(kv[:, sl], gk).astype(BF16)
    v_ref[0] = kv[:, MEM_WIDTH:].astype(BF16)


def _mem_kv(mem, g, wkv, gk):
    B, M, D = mem.shape
    full = lambda a: pl.BlockSpec(a.shape, lambda b: (0,) * a.ndim)
    out = jax.ShapeDtypeStruct((B, M, MEM_WIDTH), BF16)
    ospec = pl.BlockSpec((1, M, MEM_WIDTH), lambda b: (b, 0, 0))
    return pl.pallas_call(
        _mem_kv_kernel,
        grid=(B,),
        in_specs=[pl.BlockSpec((1, M, D), lambda b: (b, 0, 0)), full(g), full(wkv), full(gk)],
        out_specs=(ospec, ospec),
        out_shape=(out, out),
        compiler_params=_params(("parallel",)),
        name="mem_kv",
    )(mem, g, wkv, gk)


def _post_mixer_kernel(x_ref, a_ref, bT_ref, ga_ref, gb_ref, wout_ref, gmem_ref, wq_ref, gmq_ref,
                       km_ref, vm_ref, wo_ref, o_ref):
    x = x_ref[0]
    an = _rms(a_ref[0], ga_ref[...]).astype(BF16)
    bn = _rms(bT_ref[0].T, gb_ref[...]).astype(BF16)
    x1 = x + _dot(an, wout_ref[:NA_WIDTH, :]) + _dot(bn, wout_ref[NA_WIDTH:, :])

    hq = _rms(x1, gmem_ref[...]).astype(BF16)
    q = _dot(hq, wq_ref[...])
    gq = gmq_ref[...] * (MEM_HEAD_DIM ** -0.5)
    outs = []
    for h in range(MEM_HEADS):
        sl = slice(h * LANES, (h + 1) * LANES)
        qn = _rms(q[:, sl], gq).astype(BF16)
        s = _dot_nt(qn, km_ref[0, :, sl])
        m = jnp.max(s, axis=-1, keepdims=True)
        p = jnp.exp(s - m)
        l = jnp.sum(p, axis=-1, keepdims=True)
        outs.append((_dot(p.astype(BF16), vm_ref[0, :, sl]) / l).astype(BF16))
    o_ref[0] = x1 + _dot(jnp.concatenate(outs, axis=-1), wo_ref[...])


def _post_mixer(x, a, bT, ga, gb, wout, gmem, wq, gmq, km, vm, wo, *, tm):
    B, S, D = x.shape
    M = km.shape[1]
    full = lambda arr: pl.BlockSpec(arr.shape, lambda b, i: (0,) * arr.ndim)
    tok = lambda w: pl.BlockSpec((1, tm, w), lambda b, i: (b, i, 0))
    memspec = pl.BlockSpec((1, M, MEM_WIDTH), lambda b, i: (b, 0, 0))
    return pl.pallas_call(
        _post_mixer_kernel,
        grid=(B, S // tm),
        in_specs=[tok(D), tok(NA_WIDTH), pl.BlockSpec((1, MLA_WIDTH, tm), lambda b, i: (b, 0, i)),
                  full(ga), full(gb), full(wout), full(gmem), full(wq), full(gmq), memspec, memspec, full(wo)],
        out_specs=tok(D),
        out_shape=jax.ShapeDtypeStruct((B, S, D), F32),
        compiler_params=_params(("parallel", "parallel")),
        name="post_mixer",
    )(x, a, bT, ga, gb, wout, gmem, wq, gmq, km, vm, wo)


HALO = 8


def _ffn_kernel(x_ref, xp_ref, xn_ref, g_ref, wg_ref, wv_ref, cwg_ref, cwv_ref, cbg_ref, cbv_ref, wd_ref,
                o_ref, hn_ref, hh_ref, acc_ref, *, tiles_per_seq):
    i = pl.program_id(0)
    j = pl.program_id(1)
    tm = x_ref.shape[0]
    tn = wg_ref.shape[1]

    @pl.when(j == 0)
    def _():
        g = g_ref[...]
        hn_ref[...] = _rms(x_ref[...], g).astype(BF16)
        hh_ref[:HALO, :] = _rms(xp_ref[...], g).astype(BF16)
        hh_ref[HALO:, :] = _rms(xn_ref[...], g).astype(BF16)
        acc_ref[...] = jnp.zeros_like(acc_ref)

    hn = hn_ref[...]
    hh = hh_ref[...]
    has_prev = (i % tiles_per_seq != 0).astype(F32)
    has_next = ((i + 1) % tiles_per_seq != 0).astype(F32)
    row = lax.broadcasted_iota(jnp.int32, (tm, tn), 0)

    def conv(w_ref, cw_ref, cb_ref):
        w = w_ref[...]
        u = _dot(hn, w)
        uh = _dot(hh, w)
        prev = uh[HALO - 1:HALO, :] * has_prev
        nxt = uh[HALO:HALO + 1, :] * has_next
        um = jnp.where(row == 0, prev, pltpu.roll(u, 1, 0))
        up = jnp.where(row == tm - 1, nxt, pltpu.roll(u, tm - 1, 0))
        return um * cw_ref[0:1, :] + u * cw_ref[1:2, :] + up * cw_ref[2:3, :] + cb_ref[...]

    gate = conv(wg_ref, cwg_ref, cbg_ref)
    val = conv(wv_ref, cwv_ref, cbv_ref)
    act = gate / (1.0 + jnp.exp(-gate)) * val
    acc_ref[...] += _dot(act.astype(BF16), wd_ref[...])

    @pl.when(j == pl.num_programs(1) - 1)
    def _():
        o_ref[...] = x_ref[...] + acc_ref[...]


def _conv_ffn(x, g, wup, cw, cb, wd, *, tm, tn, seq):
    T, D = x.shape
    nj = D_FF // tn
    nhalo = T // HALO
    per = tm // HALO
    return pl.pallas_call(
        functools.partial(_ffn_kernel, tiles_per_seq=seq // tm),
        grid=(T // tm, nj),
        in_specs=[
            pl.BlockSpec((tm, D), lambda i, j: (i, 0)),
            pl.BlockSpec((HALO, D), lambda i, j: (jnp.maximum(i * per - 1, 0), 0)),
            pl.BlockSpec((HALO, D), lambda i, j: (jnp.minimum((i + 1) * per, nhalo - 1), 0)),
            pl.BlockSpec(g.shape, lambda i, j: (0, 0)),
            pl.BlockSpec((D, tn), lambda i, j: (0, j)),
            pl.BlockSpec((D, tn), lambda i, j: (0, nj + j)),
            pl.BlockSpec((3, tn), lambda i, j: (0, j)),
            pl.BlockSpec((3, tn), lambda i, j: (0, nj + j)),
            pl.BlockSpec((1, tn), lambda i, j: (0, j)),
            pl.BlockSpec((1, tn), lambda i, j: (0, nj + j)),
            pl.BlockSpec((tn, D), lambda i, j: (j, 0)),
        ],
        out_specs=pl.BlockSpec((tm, D), lambda i, j: (i, 0)),
        out_shape=jax.ShapeDtypeStruct((T, D), F32),
        scratch_shapes=[pltpu.VMEM((tm, D), BF16), pltpu.VMEM((2 * HALO, D), BF16), pltpu.VMEM((tm, D), F32)],
        compiler_params=_params(("parallel", "arbitrary")),
        name="conv_ffn",
    )(x, x, x, g, wup, wup, cw, cw, cb, cb, wd)


def _head_pad(w, heads, width):
    k = w.shape[0]
    w = w.reshape(k, heads, width)
    return jnp.pad(w, ((0, 0), (0, 0), (0, LANES - width))).reshape(k, heads * LANES)


def _rope_swap(a):
    half = a.shape[-1] // 2
    return jnp.concatenate([a[..., half:], a[..., :half]], axis=-1)


def _rope_sign():
    half = MLA_ROPE // 2
    return jnp.concatenate([-jnp.ones((half,), F32), jnp.ones((half,), F32)])


def _rope_tables_padded(seq):
    t = jnp.arange(seq)
    row = (t // GRID_W).astype(F32)
    col = (t % GRID_W).astype(F32)
    nfreq = MLA_ROPE // 4
    inv = 1.0 / (ROPE_BASE ** (jnp.arange(nfreq, dtype=F32) / nfreq))
    ang = jnp.concatenate([row[:, None] * inv, col[:, None] * inv], axis=-1)
    cos = jnp.concatenate([jnp.cos(ang)] * 2, axis=-1)
    sin = jnp.concatenate([jnp.sin(ang)] * 2, axis=-1)
    pad = LANES - MLA_QK
    cos = jnp.concatenate([jnp.ones((seq, MLA_NOPE), F32), cos, jnp.zeros((seq, pad), F32)], axis=-1)
    sin = jnp.concatenate([jnp.zeros((seq, MLA_NOPE), F32), sin, jnp.zeros((seq, pad), F32)], axis=-1)
    return cos, sin


def _layer_params(l, w_in, na_q_g, na_k_g, na_rpb, q_lat_g, kv_lat_g, w_uq, w_ukv, mla_q_g, mla_k_g):
    o1, o2, o3 = NA_WIDTH, 2 * NA_WIDTH, 3 * NA_WIDTH
    o4 = o3 + MLA_Q_RANK
    o5 = o4 + MLA_KV_RANK
    win = w_in[l]
    sign = _rope_sign()
    w_kr = win[:, o5:]
    pad_rope = lambda w: jnp.pad(w, ((0, 0), (MLA_NOPE, LANES - MLA_QK)))
    wlat = jnp.concatenate([win[:, o3:o5], pad_rope(w_kr), pad_rope(_rope_swap(w_kr) * sign)], axis=-1)

    wq3 = w_uq[l].reshape(MLA_Q_RANK, MLA_HEADS, MLA_QK)
    wq3s = jnp.concatenate([jnp.zeros_like(wq3[..., :MLA_NOPE]), _rope_swap(wq3[..., MLA_NOPE:]) * sign], axis=-1)
    wkv3 = w_ukv[l].reshape(MLA_KV_RANK, MLA_HEADS, MLA_NOPE + MLA_V)

    def gain_pair(g):
        gs = jnp.concatenate([jnp.zeros((MLA_NOPE,), F32), _rope_swap(g[MLA_NOPE:])])
        padl = lambda a: jnp.pad(a, (0, LANES - MLA_QK))[None, :]
        return padl(g), padl(gs)

    gq, gqs = gain_pair(mla_q_g[l])
    gk, gks = gain_pair(mla_k_g[l])
    return dict(
        wna=win[:, :o3].astype(BF16),
        wlat=wlat.astype(BF16),
        wuq=_head_pad(wq3.reshape(MLA_Q_RANK, -1), MLA_HEADS, MLA_QK).astype(BF16),
        wuqs=_head_pad(wq3s.reshape(MLA_Q_RANK, -1), MLA_HEADS, MLA_QK).astype(BF16),
        wuk=_head_pad(wkv3[..., :MLA_NOPE].reshape(MLA_KV_RANK, -1), MLA_HEADS, MLA_NOPE).astype(BF16),
        wuv=wkv3[..., MLA_NOPE:].reshape(MLA_KV_RANK, -1).astype(BF16),
        gnaq=jnp.tile(na_q_g[l], NA_HEADS)[None, :],
        gnak=jnp.tile(na_k_g[l], NA_HEADS)[None, :],
        gql=q_lat_g[l][None, :],
        gkvl=kv_lat_g[l][None, :],
        gq=gq, gqs=gqs, gk=gk, gks=gks,
        bias=_na_bias_table(na_rpb[l]),
    )


def _tiles(S):
    tm = min(512, S)
    return dict(tm=tm, tk=min(256, tm), mla_unroll=min(32, S // min(256, tm)), na_rows=8, na_trip=2, ffn_tm=min(1024, S), ffn_tn=256)


def kernel(x, mem, mix_norm_g, w_in, na_q_g, na_k_g, na_rpb, q_lat_g, kv_lat_g, w_uq, w_ukv, mla_q_g, mla_k_g,
           grp_out_g, w_out, mem_norm_g, mem_tok_norm_g, mem_w_q, mem_w_kv, mem_q_g, mem_k_g, mem_w_o,
           ffn_norm_g, ffn_w_up, ffn_conv_w, ffn_conv_b, ffn_w_down):
    B, S, D = x.shape
    depth = w_in.shape[0]
    t = _tiles(S)
    assert D == D_MODEL and S % (t["na_rows"] * GRID_W) == 0 and S % t["tm"] == 0 and S % t["ffn_tm"] == 0
    cos, sin = _rope_tables_padded(S)
    row = lambda a: a[None, :]
    for l in range(depth):
        p = _layer_params(l, w_in, na_q_g, na_k_g, na_rpb, q_lat_g, kv_lat_g, w_uq, w_ukv, mla_q_g, mla_k_g)
        qna, kna, vna, qT, k, vT, qbound = _mixer_in(
            x, row(mix_norm_g[l]), p["wna"], p["wlat"], p["wuq"], p["wuqs"], p["wuk"], p["wuv"],
            p["gnaq"], p["gnak"], p["gql"], p["gkvl"], p["gq"], p["gqs"], p["gk"], p["gks"], cos, sin,
            tm=t["tm"], tk=t["tk"])
        out_a = _na_attention(qna, kna, vna, p["bias"], rows_per_step=t["na_rows"], rows_per_trip=t["na_trip"])
        out_bT = _mla_attention(qbound[..., 0].transpose(0, 2, 1), qT, k, vT, unroll=t["mla_unroll"])
        km, vm = _mem_kv(mem, row(mem_tok_norm_g[l]), mem_w_kv[l].astype(BF16), row(mem_k_g[l]))
        x = _post_mixer(
            x, out_a, out_bT, row(grp_out_g[l, :NA_WIDTH]), row(grp_out_g[l, NA_WIDTH:]), w_out[l].astype(BF16),
            row(mem_norm_g[l]), mem_w_q[l].astype(BF16), row(mem_q_g[l]), km, vm, mem_w_o[l].astype(BF16),
            tm=t["tm"])
        x = _conv_ffn(
            x.reshape(B * S, D), row(ffn_norm_g[l]), ffn_w_up[l].astype(BF16), ffn_conv_w[l], row(ffn_conv_b[l]),
            ffn_w_down[l].astype(BF16), tm=t["ffn_tm"], tn=t["ffn_tn"], seq=S).reshape(B, S, D)
    return x
```

```python
import functools

import numpy as np
import jax
import jax.numpy as jnp
from jax import lax
from jax.experimental import pallas as pl
from jax.experimental.pallas import tpu as pltpu

F32 = jnp.float32
BF16 = jnp.bfloat16

D_MODEL = 1024
GRID_W = 64
EPS = 1e-6
NA_HEADS = 8
NA_HEAD_DIM = 64
NA_WIN_H = 8
NA_WIN_W = 16
NA_WIDTH = NA_HEADS * NA_HEAD_DIM
MLA_HEADS = 8
MLA_Q_RANK = 384
MLA_KV_RANK = 256
MLA_NOPE = 64
MLA_ROPE = 32
MLA_V = 64
MLA_QK = MLA_NOPE + MLA_ROPE
MLA_WIDTH = MLA_HEADS * MLA_V
ROPE_BASE = 10000.0
MEM_HEADS = 4
MEM_HEAD_DIM = 128
MEM_WIDTH = MEM_HEADS * MEM_HEAD_DIM
D_FF = 2816
LANES = 128
NA_TQ = NA_WIN_H * GRID_W
NA_CK = 4 * GRID_W
NEG = -1e30
LOG2E = 1.4426950408889634
MLA_AUG = MLA_QK
SAFE_BOUND = 50.0

VMEM_LIMIT = 48 * 1024 * 1024


def _params(sem):
    return pltpu.CompilerParams(dimension_semantics=sem, vmem_limit_bytes=VMEM_LIMIT)


def _rms(x, g):
    return x * lax.rsqrt(jnp.mean(x * x, axis=-1, keepdims=True) + EPS) * g


def _dot(a, b):
    return jnp.dot(a, b, preferred_element_type=F32)


def _dot_nt(a, b):
    return lax.dot_general(a, b, (((1,), (1,)), ((), ())), preferred_element_type=F32)


def _mixer_in_kernel(x_ref, gmix_ref, wna_ref, wlat_ref, wuq_ref, wuqs_ref, wuk_ref, wuv_ref,
                     gnaq_ref, gnak_ref, gql_ref, gkvl_ref, gq_ref, gqs_ref, gk_ref, gks_ref,
                     cos_ref, sin_ref,
                     qnT_ref, qnb_ref, kn_ref, vnT_ref, qT_ref, k_ref, vT_ref, qb_ref, *, tk):
    x = x_ref[0]
    tm = x.shape[0]
    hn = _rms(x, gmix_ref[...]).astype(BF16)

    zna = _dot(hn, wna_ref[...])
    lo = lax.broadcasted_iota(jnp.int32, (tm, LANES), 1) < NA_HEAD_DIM

    def seg_norm(z, g_ref, gscale):
        blocks = []
        for c in range(NA_WIDTH // LANES):
            sl = slice(c * LANES, (c + 1) * LANES)
            blk = z[:, sl]
            sq = blk * blk
            s_lo = jnp.sum(jnp.where(lo, sq, 0.0), axis=-1, keepdims=True)
            s_hi = jnp.sum(jnp.where(lo, 0.0, sq), axis=-1, keepdims=True)
            ms = jnp.where(lo, s_lo, s_hi) * (1.0 / NA_HEAD_DIM)
            blocks.append(blk * lax.rsqrt(ms + EPS) * (g_ref[:, sl] * gscale))
        return blocks

    qn = seg_norm(zna[:, :NA_WIDTH], gnaq_ref, NA_HEAD_DIM ** -0.5 * LOG2E)
    kn = seg_norm(zna[:, NA_WIDTH:2 * NA_WIDTH], gnak_ref, 1.0)
    kmax_na = (NA_HEAD_DIM ** 0.5) * jnp.max(jnp.abs(gnak_ref[...]), axis=-1, keepdims=True)
    na_tile_bounds = []
    for c in range(NA_WIDTH // LANES):
        for sub, qm in enumerate((jnp.where(lo, qn[c], 0.0), jnp.where(lo, 0.0, qn[c]))):
            qmT = qm.T
            bound = jnp.sqrt(jnp.sum(qmT * qmT, axis=0, keepdims=True)) * kmax_na
            qnT_ref[0, 2 * c + sub] = qmT.astype(BF16)
            qnb_ref[0, 2 * c + sub] = jnp.broadcast_to(bound, (8, tm))
            na_tile_bounds.append(jnp.broadcast_to(jnp.max(bound, axis=-1, keepdims=True), (1, LANES)))
        kn_ref[0, c] = kn[c].astype(BF16)
    vnT = zna[:, 2 * NA_WIDTH:].T.astype(BF16)
    for h in range(NA_HEADS):
        for c in range(tm // NA_CK):
            vnT_ref[0, h, c] = vnT[h * NA_HEAD_DIM:(h + 1) * NA_HEAD_DIM, c * NA_CK:(c + 1) * NA_CK]

    zlat = _dot(hn, wlat_ref[...])
    o1 = MLA_Q_RANK
    o2 = o1 + MLA_KV_RANK
    cqn = _rms(zlat[:, :o1], gql_ref[...]).astype(BF16)
    ckvn = _rms(zlat[:, o1:o2], gkvl_ref[...]).astype(BF16)
    kr = zlat[:, o2:o2 + LANES]
    krs = zlat[:, o2 + LANES:o2 + 2 * LANES]
    qpre = _dot(cqn, wuq_ref[...])
    qsw = _dot(cqn, wuqs_ref[...])
    kpre = _dot(ckvn, wuk_ref[...])
    v = _dot(ckvn, wuv_ref[...])
    cosv = cos_ref[...]
    sinv = sin_ref[...]
    gq = gq_ref[...]
    gqs = gqs_ref[...]
    gk = gk_ref[...]
    gks = gks_ref[...]
    qscale = MLA_QK ** -0.5 * LOG2E
    aug = lax.broadcasted_iota(jnp.int32, (tm, LANES), 1) == MLA_AUG
    kmax = (MLA_QK ** 0.5) * jnp.max(jnp.abs(gk), axis=-1, keepdims=True)
    tile_bounds = []
    for h in range(MLA_HEADS):
        sl = slice(h * LANES, (h + 1) * LANES)
        qb = qpre[:, sl]
        rinv = lax.rsqrt(jnp.sum(qb * qb, axis=-1, keepdims=True) * (1.0 / MLA_QK) + EPS)
        qr = ((qb * gq) * cosv + (qsw[:, sl] * gqs) * sinv) * (rinv * qscale)
        bound = jnp.sqrt(jnp.sum(qr * qr, axis=-1, keepdims=True)) * kmax
        qT_ref[0, h] = jnp.where(aug, -bound, qr).T.astype(BF16)
        tile_bounds.append(jnp.broadcast_to(jnp.max(bound, axis=0, keepdims=True), (1, LANES)))
        kb = kpre[:, sl] + kr
        rinv = lax.rsqrt(jnp.sum(kb * kb, axis=-1, keepdims=True) * (1.0 / MLA_QK) + EPS)
        kk = ((kb * gk) * cosv + (krs * gks) * sinv) * rinv
        k_ref[0, h] = jnp.where(aug, 1.0, kk).astype(BF16)
    qb_ref[0, 0] = jnp.concatenate(tile_bounds + na_tile_bounds, axis=0)
    vT = v.T.astype(BF16)
    for h in range(MLA_HEADS):
        for c in range(tm // tk):
            vT_ref[0, h, c] = vT[h * MLA_V:(h + 1) * MLA_V, c * tk:(c + 1) * tk]


def _mixer_in(x, gmix, wna, wlat, wuq, wuqs, wuk, wuv, gnaq, gnak, gql, gkvl, gq, gqs, gk, gks, cos, sin, *, tm, tk):
    B, S, D = x.shape
    nt = S // tm
    full = lambda a: pl.BlockSpec(a.shape, lambda b, i: (0,) * a.ndim)
    consts = (gmix, wna, wlat, wuq, wuqs, wuk, wuv, gnaq, gnak, gql, gkvl, gq, gqs, gk, gks)
    tok = lambda w: pl.BlockSpec((1, tm, w), lambda b, i: (b, i, 0))
    out_shape = (
        jax.ShapeDtypeStruct((B, NA_HEADS, LANES, S), BF16),
        jax.ShapeDtypeStruct((B, NA_HEADS, 8, S), F32),
        jax.ShapeDtypeStruct((B, NA_WIDTH // LANES, S, LANES), BF16),
        jax.ShapeDtypeStruct((B, NA_HEADS, S // NA_CK, NA_HEAD_DIM, NA_CK), BF16),
        jax.ShapeDtypeStruct((B, MLA_HEADS, LANES, S), BF16),
        jax.ShapeDtypeStruct((B, MLA_HEADS, S, LANES), BF16),
        jax.ShapeDtypeStruct((B, MLA_HEADS, S // tk, MLA_V, tk), BF16),
        jax.ShapeDtypeStruct((B, nt, MLA_HEADS + NA_HEADS, LANES), F32),
    )
    out_specs = (
        pl.BlockSpec((1, NA_HEADS, LANES, tm), lambda b, i: (b, 0, 0, i)),
        pl.BlockSpec((1, NA_HEADS, 8, tm), lambda b, i: (b, 0, 0, i)),
        pl.BlockSpec((1, NA_WIDTH // LANES, tm, LANES), lambda b, i: (b, 0, i, 0)),
        pl.BlockSpec((1, NA_HEADS, tm // NA_CK, NA_HEAD_DIM, NA_CK), lambda b, i: (b, 0, i, 0, 0)),
        pl.BlockSpec((1, MLA_HEADS, LANES, tm), lambda b, i: (b, 0, 0, i)),
        pl.BlockSpec((1, MLA_HEADS, tm, LANES), lambda b, i: (b, 0, i, 0)),
        pl.BlockSpec((1, MLA_HEADS, tm // tk, MLA_V, tk), lambda b, i: (b, 0, i, 0, 0)),
        pl.BlockSpec((1, 1, MLA_HEADS + NA_HEADS, LANES), lambda b, i: (b, i, 0, 0)),
    )
    return pl.pallas_call(
        functools.partial(_mixer_in_kernel, tk=tk),
        grid=(B, nt),
        in_specs=[tok(D)] + [full(a) for a in consts]
        + [pl.BlockSpec((tm, LANES), lambda b, i: (i, 0))] * 2,
        out_specs=out_specs,
        out_shape=out_shape,
        compiler_params=_params(("parallel", "parallel")),
        name="mixer_in",
    )(x, *consts, cos, sin)


def _na_kernel(guard_ref, qT_ref, qb_ref, k0_ref, k1_ref, k2_ref, k3_ref, v0_ref, v1_ref, v2_ref, v3_ref, t_ref,
               o_ref):
    tq = qT_ref.shape[3]
    k_refs = (k0_ref, k1_ref, k2_ref, k3_ref)
    v_refs = (v0_ref, v1_ref, v2_ref, v3_ref)
    tile = (pl.program_id(0) * pl.num_programs(1) + pl.program_id(1)) * pl.num_programs(2) + pl.program_id(2)
    safe = guard_ref[tile] <= 2.0 * SAFE_BOUND

    def finish(hh, l8, acc):
        o_ref[0, hh * NA_HEAD_DIM:(hh + 1) * NA_HEAD_DIM, :] = acc / jnp.sum(l8, axis=0, keepdims=True)

    def accumulate(p, v, l8, acc):
        return l8 + jnp.sum(p.reshape(NA_CK // 8, 8, tq), axis=0), acc + _dot(v, p.astype(BF16))

    @pl.when(safe)
    def _():
        for hh in range(2):
            qT = qT_ref[0, hh]
            shift = qb_ref[0, hh, 0:1, :]
            l8 = jnp.zeros((8, tq), F32)
            acc = jnp.zeros((NA_HEAD_DIM, tq), F32)
            for j, k_ref in enumerate(k_refs):
                p = jnp.exp2(_dot(k_ref[0, 0], qT) + t_ref[0, hh, j] - shift)
                l8, acc = accumulate(p, v_refs[j][0, hh, 0], l8, acc)
            finish(hh, l8, acc)

    @pl.when(jnp.logical_not(safe))
    def _():
        for hh in range(2):
            qT = qT_ref[0, hh]
            scores = [_dot(k_ref[0, 0], qT) + t_ref[0, hh, j] for j, k_ref in enumerate(k_refs)]
            m = jnp.max(jnp.maximum(jnp.maximum(scores[0], scores[1]), jnp.maximum(scores[2], scores[3])),
                        axis=0, keepdims=True)
            l8 = jnp.zeros((8, tq), F32)
            acc = jnp.zeros((NA_HEAD_DIM, tq), F32)
            for j, sc in enumerate(scores):
                l8, acc = accumulate(jnp.exp2(sc - m), v_refs[j][0, hh, 0], l8, acc)
            finish(hh, l8, acc)


def _na_attention(guard, qT, qb, k, vT, table):
    B, H, _, S = qT.shape
    G = S // NA_TQ
    nck = S // NA_CK
    per = NA_TQ // NA_CK

    def chunk(g, j):
        return jnp.clip(g * per - 1 + j, 0, nck - 1)

    kspec = lambda j: pl.BlockSpec((1, 1, NA_CK, LANES), lambda b, hp, g: (b, hp, chunk(g, j), 0))
    vspec = lambda j: pl.BlockSpec((1, 2, 1, NA_HEAD_DIM, NA_CK), lambda b, hp, g: (b, hp, chunk(g, j), 0, 0))
    variant = lambda g: jnp.where(g == 0, 0, jnp.where(g == G - 1, 2, 1))
    return pl.pallas_call(
        _na_kernel,
        grid=(B, H // 2, G),
        in_specs=[pl.BlockSpec(memory_space=pltpu.SMEM),
                  pl.BlockSpec((1, 2, LANES, NA_TQ), lambda b, hp, g: (b, hp, 0, g)),
                  pl.BlockSpec((1, 2, 8, NA_TQ), lambda b, hp, g: (b, hp, 0, g))]
        + [kspec(j) for j in range(4)] + [vspec(j) for j in range(4)]
        + [pl.BlockSpec((1, 2) + table.shape[2:], lambda b, hp, g: (variant(g), hp, 0, 0, 0))],
        out_specs=pl.BlockSpec((1, 2 * NA_HEAD_DIM, NA_TQ), lambda b, hp, g: (b, hp, g)),
        out_shape=jax.ShapeDtypeStruct((B, H * NA_HEAD_DIM, S), F32),
        compiler_params=_params(("parallel", "parallel", "parallel")),
        name="na_attn",
    )(guard.reshape(-1), qT, qb, k, k, k, k, vT, vT, vT, vT, table)


def _na_bias_table(rpb, n_rows):
    G = n_rows // NA_WIN_H
    col = jnp.arange(GRID_W)
    c0 = jnp.clip(col - NA_WIN_W // 2, 0, GRID_W - NA_WIN_W)
    inwin = (col[:, None] >= c0[None, :]) & (col[:, None] < c0[None, :] + NA_WIN_W)
    dc = jnp.clip(col[:, None] - col[None, :] + (NA_WIN_W - 1), 0, 2 * NA_WIN_W - 2)
    shifted = (rpb - jnp.max(rpb, axis=(1, 2), keepdims=True)) * LOG2E
    b2 = jnp.where(inwin[None, None], shifted[:, :, dc], NEG)
    dr = np.zeros((3, 4, 4, NA_WIN_H), np.int32)
    valid = np.zeros((3, 4, 4, NA_WIN_H), bool)
    for v, g in enumerate((0, 1, G - 1)):
        for j in range(4):
            for kk in range(4):
                for e in range(NA_WIN_H):
                    kr = NA_WIN_H * g - 4 + 4 * j + kk
                    qr = NA_WIN_H * g + e
                    start = min(max(qr - NA_WIN_H // 2, 0), n_rows - NA_WIN_H)
                    if 0 <= kr < n_rows and start <= kr < start + NA_WIN_H:
                        valid[v, j, kk, e] = True
                        dr[v, j, kk, e] = kr - qr + (NA_WIN_H - 1)
    t = jnp.where(valid[None, ..., None, None], b2[:, dr], NEG)
    t = t.transpose(1, 0, 2, 3, 5, 4, 6)
    return t.reshape(3, rpb.shape[0], 4, NA_CK, NA_TQ).astype(F32)


def _mla_kernel(bound_ref, qT_ref, k_ref, vT_ref, o_ref, *, tk, unroll):
    qa = qT_ref[0, 0]
    tq = qa.shape[1]
    nk = k_ref.shape[2] // tk
    tile = (pl.program_id(0) * pl.num_programs(1) + pl.program_id(1)) * pl.num_programs(2) + pl.program_id(2)
    safe = bound_ref[tile] <= SAFE_BOUND

    def scores(c):
        off = pl.multiple_of(c * tk, tk)
        return _dot(k_ref[0, 0, pl.ds(off, tk), :], qa)

    @pl.when(safe)
    def _():
        def body(t, carry):
            l8, acc = carry
            c0 = t * unroll
            s = scores(c0)
            for u in range(unroll):
                s_next = scores(c0 + u + 1) if u + 1 < unroll else None
                p = jnp.exp2(s)
                l8 = l8 + jnp.sum(p.reshape(tk // 8, 8, tq), axis=0)
                acc = acc + _dot(vT_ref[0, 0, c0 + u], p.astype(BF16))
                s = s_next
            return l8, acc

        init = (jnp.zeros((8, tq), F32), jnp.zeros((MLA_V, tq), F32))
        l8, acc = lax.fori_loop(0, nk // unroll, body, init)
        o_ref[0] = acc / jnp.sum(l8, axis=0, keepdims=True)

    @pl.when(jnp.logical_not(safe))
    def _():
        def body(c, carry):
            m, l, acc = carry
            s = scores(c)
            m_new = jnp.maximum(m, jnp.max(s, axis=0, keepdims=True))
            alpha = jnp.exp2(m - m_new)
            p = jnp.exp2(s - m_new)
            l = alpha * l + jnp.sum(p, axis=0, keepdims=True)
            acc = alpha * acc + _dot(vT_ref[0, 0, c], p.astype(BF16))
            return m_new, l, acc

        init = (jnp.full((1, tq), NEG, F32), jnp.zeros((1, tq), F32), jnp.zeros((MLA_V, tq), F32))
        m, l, acc = lax.fori_loop(0, nk, body, init)
        o_ref[0] = acc / l


def _mla_attention(tile_bound, qT, k, vT, *, unroll):
    B, H, _, S = qT.shape
    tq = S // tile_bound.shape[2]
    tk = vT.shape[-1]
    return pl.pallas_call(
        functools.partial(_mla_kernel, tk=tk, unroll=unroll),
        grid=(B, H, S // tq),
        in_specs=[
            pl.BlockSpec(memory_space=pltpu.SMEM),
            pl.BlockSpec((1, 1, LANES, tq), lambda b, h, i: (b, h, 0, i)),
            pl.BlockSpec((1, 1, S, LANES), lambda b, h, i: (b, h, 0, 0)),
            pl.BlockSpec((1, 1, S // tk, MLA_V, tk), lambda b, h, i: (b, h, 0, 0, 0)),
        ],
        out_specs=pl.BlockSpec((1, MLA_V, tq), lambda b, h, i: (b, h, i)),
        out_shape=jax.ShapeDtypeStruct((B, H * MLA_V, S), F32),
        compiler_params=_params(("parallel", "parallel", "parallel")),
        name="mla_attn",
    )(tile_bound.reshape(-1), qT, k, vT)


def _mem_kv_kernel(mem_ref, g_ref, wkv_ref, gk_ref, k_ref, v_ref):
    mn = _rms(mem_ref[0], g_ref[...]).astype(BF16)
    kv = _dot(mn, wkv_ref[...])
    gk = gk_ref[...]
    for h in range(MEM_HEADS):
        sl = slice(h * LANES, (h + 1) * LANES)
        k_ref[0, :, sl] = _rms(kv[:, sl], gk).astype(BF16)
    v_ref[0] = kv[:, MEM_WIDTH:].astype(BF16)


def _mem_kv(mem, g, wkv, gk):
    B, M, D = mem.shape
    full = lambda a: pl.BlockSpec(a.shape, lambda b: (0,) * a.ndim)
    out = jax.ShapeDtypeStruct((B, M, MEM_WIDTH), BF16)
    ospec = pl.BlockSpec((1, M, MEM_WIDTH), lambda b: (b, 0, 0))
    return pl.pallas_call(
        _mem_kv_kernel,
        grid=(B,),
        in_specs=[pl.BlockSpec((1, M, D), lambda b: (b, 0, 0)), full(g), full(wkv), full(gk)],
        out_specs=(ospec, ospec),
        out_shape=(out, out),
        compiler_params=_params(("parallel",)),
        name="mem_kv",
    )(mem, g, wkv, gk)


def _post_mixer_kernel(x_ref, aT_ref, bT_ref, ga_ref, gb_ref, wout_ref, gmem_ref, wq_ref, gmq_ref,
                       km_ref, vm_ref, wo_ref, o_ref):
    x = x_ref[0]
    an = _rms(aT_ref[0].T, ga_ref[...]).astype(BF16)
    bn = _rms(bT_ref[0].T, gb_ref[...]).astype(BF16)
    x1 = x + _dot(an, wout_ref[:NA_WIDTH, :]) + _dot(bn, wout_ref[NA_WIDTH:, :])

    hq = _rms(x1, gmem_ref[...]).astype(BF16)
    q = _dot(hq, wq_ref[...])
    gq = gmq_ref[...] * (MEM_HEAD_DIM ** -0.5)
    outs = []
    for h in range(MEM_HEADS):
        sl = slice(h * LANES, (h + 1) * LANES)
        qn = _rms(q[:, sl], gq).astype(BF16)
        s = _dot_nt(qn, km_ref[0, :, sl])
        m = jnp.max(s, axis=-1, keepdims=True)
        p = jnp.exp(s - m)
        l = jnp.sum(p, axis=-1, keepdims=True)
        outs.append((_dot(p.astype(BF16), vm_ref[0, :, sl]) / l).astype(BF16))
    o_ref[0] = x1 + _dot(jnp.concatenate(outs, axis=-1), wo_ref[...])


def _post_mixer(x, aT, bT, ga, gb, wout, gmem, wq, gmq, km, vm, wo, *, tm):
    B, S, D = x.shape
    M = km.shape[1]
    full = lambda arr: pl.BlockSpec(arr.shape, lambda b, i: (0,) * arr.ndim)
    tok = lambda w: pl.BlockSpec((1, tm, w), lambda b, i: (b, i, 0))
    memspec = pl.BlockSpec((1, M, MEM_WIDTH), lambda b, i: (b, 0, 0))
    return pl.pallas_call(
        _post_mixer_kernel,
        grid=(B, S // tm),
        in_specs=[tok(D), pl.BlockSpec((1, NA_WIDTH, tm), lambda b, i: (b, 0, i)),
                  pl.BlockSpec((1, MLA_WIDTH, tm), lambda b, i: (b, 0, i)),
                  full(ga), full(gb), full(wout), full(gmem), full(wq), full(gmq), memspec, memspec, full(wo)],
        out_specs=tok(D),
        out_shape=jax.ShapeDtypeStruct((B, S, D), F32),
        compiler_params=_params(("parallel", "parallel")),
        name="post_mixer",
    )(x, aT, bT, ga, gb, wout, gmem, wq, gmq, km, vm, wo)


HALO = 8


def _ffn_kernel(x_ref, xp_ref, xn_ref, g_ref, wg_ref, wv_ref, cwg_ref, cwv_ref, cbg_ref, cbv_ref, wd_ref,
                o_ref, hn_ref, hh_ref, acc_ref, *, tiles_per_seq):
    i = pl.program_id(0)
    j = pl.program_id(1)
    tm = x_ref.shape[0]
    tn = wg_ref.shape[1]

    @pl.when(j == 0)
    def _():
        g = g_ref[...]
        hn_ref[...] = _rms(x_ref[...], g).astype(BF16)
        hh_ref[:HALO, :] = _rms(xp_ref[...], g).astype(BF16)
        hh_ref[HALO:, :] = _rms(xn_ref[...], g).astype(BF16)
        acc_ref[...] = jnp.zeros_like(acc_ref)

    hn = hn_ref[...]
    hh = hh_ref[...]
    has_prev = (i % tiles_per_seq != 0).astype(F32)
    has_next = ((i + 1) % tiles_per_seq != 0).astype(F32)
    row = lax.broadcasted_iota(jnp.int32, (tm, tn), 0)

    def conv(w_ref, cw_ref, cb_ref):
        w = w_ref[...]
        u = _dot(hn, w)
        uh = _dot(hh, w)
        prev = uh[HALO - 1:HALO, :] * has_prev
        nxt = uh[HALO:HALO + 1, :] * has_next
        um = jnp.where(row == 0, prev, pltpu.roll(u, 1, 0))
        up = jnp.where(row == tm - 1, nxt, pltpu.roll(u, tm - 1, 0))
        return um * cw_ref[0:1, :] + u * cw_ref[1:2, :] + up * cw_ref[2:3, :] + cb_ref[...]

    gate = conv(wg_ref, cwg_ref, cbg_ref)
    val = conv(wv_ref, cwv_ref, cbv_ref)
    act = gate / (1.0 + jnp.exp(-gate)) * val
    acc_ref[...] += _dot(act.astype(BF16), wd_ref[...])

    @pl.when(j == pl.num_programs(1) - 1)
    def _():
        o_ref[...] = x_ref[...] + acc_ref[...]


def _conv_ffn(x, g, wup, cw, cb, wd, *, tm, tn, seq):
    T, D = x.shape
    nj = D_FF // tn
    nhalo = T // HALO
    per = tm // HALO
    return pl.pallas_call(
        functools.partial(_ffn_kernel, tiles_per_seq=seq // tm),
        grid=(T // tm, nj),
        in_specs=[
            pl.BlockSpec((tm, D), lambda i, j: (i, 0)),
            pl.BlockSpec((HALO, D), lambda i, j: (jnp.maximum(i * per - 1, 0), 0)),
            pl.BlockSpec((HALO, D), lambda i, j: (jnp.minimum((i + 1) * per, nhalo - 1), 0)),
            pl.BlockSpec(g.shape, lambda i, j: (0, 0)),
            pl.BlockSpec((D, tn), lambda i, j: (0, j)),
            pl.BlockSpec((D, tn), lambda i, j: (0, nj + j)),
            pl.BlockSpec((3, tn), lambda i, j: (0, j)),
            pl.BlockSpec((3, tn), lambda i, j: (0, nj + j)),
            pl.BlockSpec((1, tn), lambda i, j: (0, j)),
            pl.BlockSpec((1, tn), lambda i, j: (0, nj + j)),
            pl.BlockSpec((tn, D), lambda i, j: (j, 0)),
        ],
        out_specs=pl.BlockSpec((tm, D), lambda i, j: (i, 0)),
        out_shape=jax.ShapeDtypeStruct((T, D), F32),
        scratch_shapes=[pltpu.VMEM((tm, D), BF16), pltpu.VMEM((2 * HALO, D), BF16), pltpu.VMEM((tm, D), F32)],
        compiler_params=_params(("parallel", "arbitrary")),
        name="conv_ffn",
    )(x, x, x, g, wup, wup, cw, cw, cb, cb, wd)


def _head_pad(w, heads, width):
    k = w.shape[0]
    w = w.reshape(k, heads, width)
    return jnp.pad(w, ((0, 0), (0, 0), (0, LANES - width))).reshape(k, heads * LANES)


def _rope_swap(a):
    half = a.shape[-1] // 2
    return jnp.concatenate([a[..., half:], a[..., :half]], axis=-1)


def _rope_sign():
    half = MLA_ROPE // 2
    return jnp.concatenate([-jnp.ones((half,), F32), jnp.ones((half,), F32)])


def _rope_tables_padded(seq):
    t = jnp.arange(seq)
    row = (t // GRID_W).astype(F32)
    col = (t % GRID_W).astype(F32)
    nfreq = MLA_ROPE // 4
    inv = 1.0 / (ROPE_BASE ** (jnp.arange(nfreq, dtype=F32) / nfreq))
    ang = jnp.concatenate([row[:, None] * inv, col[:, None] * inv], axis=-1)
    cos = jnp.concatenate([jnp.cos(ang)] * 2, axis=-1)
    sin = jnp.concatenate([jnp.sin(ang)] * 2, axis=-1)
    pad = LANES - MLA_QK
    cos = jnp.concatenate([jnp.ones((seq, MLA_NOPE), F32), cos, jnp.zeros((seq, pad), F32)], axis=-1)
    sin = jnp.concatenate([jnp.zeros((seq, MLA_NOPE), F32), sin, jnp.zeros((seq, pad), F32)], axis=-1)
    return cos, sin


def _layer_params(l, n_rows, w_in, na_q_g, na_k_g, na_rpb, q_lat_g, kv_lat_g, w_uq, w_ukv, mla_q_g, mla_k_g):
    o1, o2, o3 = NA_WIDTH, 2 * NA_WIDTH, 3 * NA_WIDTH
    o4 = o3 + MLA_Q_RANK
    o5 = o4 + MLA_KV_RANK
    win = w_in[l]
    sign = _rope_sign()
    w_kr = win[:, o5:]
    pad_rope = lambda w: jnp.pad(w, ((0, 0), (MLA_NOPE, LANES - MLA_QK)))
    wlat = jnp.concatenate([win[:, o3:o5], pad_rope(w_kr), pad_rope(_rope_swap(w_kr) * sign)], axis=-1)

    wq3 = w_uq[l].reshape(MLA_Q_RANK, MLA_HEADS, MLA_QK)
    wq3s = jnp.concatenate([jnp.zeros_like(wq3[..., :MLA_NOPE]), _rope_swap(wq3[..., MLA_NOPE:]) * sign], axis=-1)
    wkv3 = w_ukv[l].reshape(MLA_KV_RANK, MLA_HEADS, MLA_NOPE + MLA_V)

    def gain_pair(g):
        gs = jnp.concatenate([jnp.zeros((MLA_NOPE,), F32), _rope_swap(g[MLA_NOPE:])])
        padl = lambda a: jnp.pad(a, (0, LANES - MLA_QK))[None, :]
        return padl(g), padl(gs)

    gq, gqs = gain_pair(mla_q_g[l])
    gk, gks = gain_pair(mla_k_g[l])
    return dict(
        wna=win[:, :o3].astype(BF16),
        wlat=wlat.astype(BF16),
        wuq=_head_pad(wq3.reshape(MLA_Q_RANK, -1), MLA_HEADS, MLA_QK).astype(BF16),
        wuqs=_head_pad(wq3s.reshape(MLA_Q_RANK, -1), MLA_HEADS, MLA_QK).astype(BF16),
        wuk=_head_pad(wkv3[..., :MLA_NOPE].reshape(MLA_KV_RANK, -1), MLA_HEADS, MLA_NOPE).astype(BF16),
        wuv=wkv3[..., MLA_NOPE:].reshape(MLA_KV_RANK, -1).astype(BF16),
        gnaq=jnp.tile(na_q_g[l], NA_HEADS)[None, :],
        gnak=jnp.tile(na_k_g[l], NA_HEADS)[None, :],
        gql=q_lat_g[l][None, :],
        gkvl=kv_lat_g[l][None, :],
        gq=gq, gqs=gqs, gk=gk, gks=gks,
        bias=_na_bias_table(na_rpb[l], n_rows),
    )


def _tiles(S):
    tm = min(512, S)
    return dict(tm=tm, tk=min(512, tm), mla_unroll=S // min(512, tm), ffn_tm=min(1024, S), ffn_tn=1408)


def kernel(x, mem, mix_norm_g, w_in, na_q_g, na_k_g, na_rpb, q_lat_g, kv_lat_g, w_uq, w_ukv, mla_q_g, mla_k_g,
           grp_out_g, w_out, mem_norm_g, mem_tok_norm_g, mem_w_q, mem_w_kv, mem_q_g, mem_k_g, mem_w_o,
           ffn_norm_g, ffn_w_up, ffn_conv_w, ffn_conv_b, ffn_w_down):
    B, S, D = x.shape
    depth = w_in.shape[0]
    t = _tiles(S)
    assert D == D_MODEL and S >= 2 * NA_TQ and t["tm"] == NA_TQ and S % t["tm"] == 0 and S % t["ffn_tm"] == 0
    cos, sin = _rope_tables_padded(S)
    row = lambda a: a[None, :]
    for l in range(depth):
        p = _layer_params(l, S // GRID_W, w_in, na_q_g, na_k_g, na_rpb, q_lat_g, kv_lat_g, w_uq, w_ukv, mla_q_g, mla_k_g)
        qnT, qnb, kn, vnT, qT, k, vT, qbound = _mixer_in(
            x, row(mix_norm_g[l]), p["wna"], p["wlat"], p["wuq"], p["wuqs"], p["wuk"], p["wuv"],
            p["gnaq"], p["gnak"], p["gql"], p["gkvl"], p["gq"], p["gqs"], p["gk"], p["gks"], cos, sin,
            tm=t["tm"], tk=t["tk"])
        tb = qbound[..., 0]
        rpb_range = (jnp.max(na_rpb[l], axis=(1, 2)) - jnp.min(na_rpb[l], axis=(1, 2))) * LOG2E
        na_guard = (2.0 * tb[:, :, MLA_HEADS:] + rpb_range).reshape(B, -1, NA_HEADS // 2, 2).max(axis=-1)
        out_aT = _na_attention(na_guard.transpose(0, 2, 1), qnT, qnb, kn, vnT, p["bias"])
        out_bT = _mla_attention(tb[:, :, :MLA_HEADS].transpose(0, 2, 1), qT, k, vT, unroll=t["mla_unroll"])
        km, vm = _mem_kv(mem, row(mem_tok_norm_g[l]), mem_w_kv[l].astype(BF16), row(mem_k_g[l]))
        x = _post_mixer(
            x, out_aT, out_bT, row(grp_out_g[l, :NA_WIDTH]), row(grp_out_g[l, NA_WIDTH:]), w_out[l].astype(BF16),
            row(mem_norm_g[l]), mem_w_q[l].astype(BF16), row(mem_q_g[l]), km, vm, mem_w_o[l].astype(BF16),
            tm=t["tm"])
        x = _conv_ffn(
            x.reshape(B * S, D), row(ffn_norm_g[l]), ffn_w_up[l].astype(BF16), ffn_conv_w[l], row(ffn_conv_b[l]),
            ffn_w_down[l].astype(BF16), tm=t["ffn_tm"], tn=t["ffn_tn"], seq=S).reshape(B, S, D)
    return x
```

```python
import functools

import numpy as np
import jax
import jax.numpy as jnp
from jax import lax
from jax.experimental import pallas as pl
from jax.experimental.pallas import tpu as pltpu

F32 = jnp.float32
BF16 = jnp.bfloat16

D_MODEL = 1024
GRID_W = 64
EPS = 1e-6
NA_HEADS = 8
NA_HEAD_DIM = 64
NA_WIN_H = 8
NA_WIN_W = 16
NA_WIDTH = NA_HEADS * NA_HEAD_DIM
MLA_HEADS = 8
MLA_Q_RANK = 384
MLA_KV_RANK = 256
MLA_NOPE = 64
MLA_ROPE = 32
MLA_V = 64
MLA_QK = MLA_NOPE + MLA_ROPE
MLA_WIDTH = MLA_HEADS * MLA_V
ROPE_BASE = 10000.0
MEM_HEADS = 4
MEM_HEAD_DIM = 128
MEM_WIDTH = MEM_HEADS * MEM_HEAD_DIM
D_FF = 2816
LANES = 128
NA_TQ = NA_WIN_H * GRID_W
NA_CK = 4 * GRID_W
NA_HPS = 4
NEG = -1e30
LOG2E = 1.4426950408889634
MLA_AUG = MLA_QK
SAFE_BOUND = 50.0

VMEM_LIMIT = 48 * 1024 * 1024


def _params(sem):
    return pltpu.CompilerParams(dimension_semantics=sem, vmem_limit_bytes=VMEM_LIMIT)


def _rms(x, g):
    return x * lax.rsqrt(jnp.mean(x * x, axis=-1, keepdims=True) + EPS) * g


def _dot(a, b):
    return jnp.dot(a, b, preferred_element_type=F32)


def _dot_nt(a, b):
    return lax.dot_general(a, b, (((1,), (1,)), ((), ())), preferred_element_type=F32)


def _mixer_in_kernel(x_ref, gmix_ref, wna_ref, wlat_ref, wuq_ref, wuqs_ref, wuk_ref, wuv_ref,
                     gnaq_ref, gnak_ref, gql_ref, gkvl_ref, gq_ref, gqs_ref, gk_ref, gks_ref,
                     cos_ref, sin_ref,
                     qnT_ref, qnb_ref, kn_ref, vnT_ref, qT_ref, k_ref, vT_ref, qb_ref, *, tk):
    x = x_ref[0]
    tm = x.shape[0]
    hn = _rms(x, gmix_ref[...]).astype(BF16)

    zna = _dot(hn, wna_ref[...])
    lo = lax.broadcasted_iota(jnp.int32, (tm, LANES), 1) < NA_HEAD_DIM

    def seg_norm(z, g_ref, gscale):
        blocks = []
        for c in range(NA_WIDTH // LANES):
            sl = slice(c * LANES, (c + 1) * LANES)
            blk = z[:, sl]
            sq = blk * blk
            s_lo = jnp.sum(jnp.where(lo, sq, 0.0), axis=-1, keepdims=True)
            s_hi = jnp.sum(jnp.where(lo, 0.0, sq), axis=-1, keepdims=True)
            ms = jnp.where(lo, s_lo, s_hi) * (1.0 / NA_HEAD_DIM)
            blocks.append(blk * lax.rsqrt(ms + EPS) * (g_ref[:, sl] * gscale))
        return blocks

    qn = seg_norm(zna[:, :NA_WIDTH], gnaq_ref, NA_HEAD_DIM ** -0.5 * LOG2E)
    kn = seg_norm(zna[:, NA_WIDTH:2 * NA_WIDTH], gnak_ref, 1.0)
    kmax_na = (NA_HEAD_DIM ** 0.5) * jnp.max(jnp.abs(gnak_ref[...]), axis=-1, keepdims=True)
    na_tile_bounds = []
    for c in range(NA_WIDTH // LANES):
        for sub, qm in enumerate((jnp.where(lo, qn[c], 0.0), jnp.where(lo, 0.0, qn[c]))):
            qmT = qm.T
            bound = jnp.sqrt(jnp.sum(qmT * qmT, axis=0, keepdims=True)) * kmax_na
            qnT_ref[0, 2 * c + sub] = qmT.astype(BF16)
            qnb_ref[0, 2 * c + sub] = jnp.broadcast_to(bound, (8, tm))
            na_tile_bounds.append(jnp.broadcast_to(jnp.max(bound, axis=-1, keepdims=True), (1, LANES)))
        kn_ref[0, c] = kn[c].astype(BF16)
    vnT = zna[:, 2 * NA_WIDTH:].T.astype(BF16)
    for h in range(NA_HEADS):
        for c in range(tm // NA_CK):
            vnT_ref[0, h, c] = vnT[h * NA_HEAD_DIM:(h + 1) * NA_HEAD_DIM, c * NA_CK:(c + 1) * NA_CK]

    zlat = _dot(hn, wlat_ref[...])
    o1 = MLA_Q_RANK
    o2 = o1 + MLA_KV_RANK
    cqn = _rms(zlat[:, :o1], gql_ref[...]).astype(BF16)
    ckvn = _rms(zlat[:, o1:o2], gkvl_ref[...]).astype(BF16)
    kr = zlat[:, o2:o2 + LANES]
    krs = zlat[:, o2 + LANES:o2 + 2 * LANES]
    qpre = _dot(cqn, wuq_ref[...])
    qsw = _dot(cqn, wuqs_ref[...])
    kpre = _dot(ckvn, wuk_ref[...])
    v = _dot(ckvn, wuv_ref[...])
    cosv = cos_ref[...]
    sinv = sin_ref[...]
    gq = gq_ref[...]
    gqs = gqs_ref[...]
    gk = gk_ref[...]
    gks = gks_ref[...]
    qscale = MLA_QK ** -0.5 * LOG2E
    aug = lax.broadcasted_iota(jnp.int32, (tm, LANES), 1) == MLA_AUG
    kmax = (MLA_QK ** 0.5) * jnp.max(jnp.abs(gk), axis=-1, keepdims=True)
    tile_bounds = []
    for h in range(MLA_HEADS):
        sl = slice(h * LANES, (h + 1) * LANES)
        qb = qpre[:, sl]
        rinv = lax.rsqrt(jnp.sum(qb * qb, axis=-1, keepdims=True) * (1.0 / MLA_QK) + EPS)
        qr = ((qb * gq) * cosv + (qsw[:, sl] * gqs) * sinv) * (rinv * qscale)
        bound = jnp.sqrt(jnp.sum(qr * qr, axis=-1, keepdims=True)) * kmax
        qT_ref[0, h] = jnp.where(aug, -bound, qr).T.astype(BF16)
        tile_bounds.append(jnp.broadcast_to(jnp.max(bound, axis=0, keepdims=True), (1, LANES)))
        kb = kpre[:, sl] + kr
        rinv = lax.rsqrt(jnp.sum(kb * kb, axis=-1, keepdims=True) * (1.0 / MLA_QK) + EPS)
        kk = ((kb * gk) * cosv + (krs * gks) * sinv) * rinv
        k_ref[0, h] = jnp.where(aug, 1.0, kk).astype(BF16)
    qb_ref[0, 0] = jnp.concatenate(tile_bounds + na_tile_bounds, axis=0)
    vT = v.T.astype(BF16)
    for h in range(MLA_HEADS):
        for c in range(tm // tk):
            vT_ref[0, h, c] = vT[h * MLA_V:(h + 1) * MLA_V, c * tk:(c + 1) * tk]


def _mixer_in(x, gmix, wna, wlat, wuq, wuqs, wuk, wuv, gnaq, gnak, gql, gkvl, gq, gqs, gk, gks, cos, sin, *, tm, tk):
    B, S, D = x.shape
    nt = S // tm
    full = lambda a: pl.BlockSpec(a.shape, lambda b, i: (0,) * a.ndim)
    consts = (gmix, wna, wlat, wuq, wuqs, wuk, wuv, gnaq, gnak, gql, gkvl, gq, gqs, gk, gks)
    tok = lambda w: pl.BlockSpec((1, tm, w), lambda b, i: (b, i, 0))
    out_shape = (
        jax.ShapeDtypeStruct((B, NA_HEADS, LANES, S), BF16),
        jax.ShapeDtypeStruct((B, NA_HEADS, 8, S), F32),
        jax.ShapeDtypeStruct((B, NA_WIDTH // LANES, S, LANES), BF16),
        jax.ShapeDtypeStruct((B, NA_HEADS, S // NA_CK, NA_HEAD_DIM, NA_CK), BF16),
        jax.ShapeDtypeStruct((B, MLA_HEADS, LANES, S), BF16),
        jax.ShapeDtypeStruct((B, MLA_HEADS, S, LANES), BF16),
        jax.ShapeDtypeStruct((B, MLA_HEADS, S // tk, MLA_V, tk), BF16),
        jax.ShapeDtypeStruct((B, nt, MLA_HEADS + NA_HEADS, LANES), F32),
    )
    out_specs = (
        pl.BlockSpec((1, NA_HEADS, LANES, tm), lambda b, i: (b, 0, 0, i)),
        pl.BlockSpec((1, NA_HEADS, 8, tm), lambda b, i: (b, 0, 0, i)),
        pl.BlockSpec((1, NA_WIDTH // LANES, tm, LANES), lambda b, i: (b, 0, i, 0)),
        pl.BlockSpec((1, NA_HEADS, tm // NA_CK, NA_HEAD_DIM, NA_CK), lambda b, i: (b, 0, i, 0, 0)),
        pl.BlockSpec((1, MLA_HEADS, LANES, tm), lambda b, i: (b, 0, 0, i)),
        pl.BlockSpec((1, MLA_HEADS, tm, LANES), lambda b, i: (b, 0, i, 0)),
        pl.BlockSpec((1, MLA_HEADS, tm // tk, MLA_V, tk), lambda b, i: (b, 0, i, 0, 0)),
        pl.BlockSpec((1, 1, MLA_HEADS + NA_HEADS, LANES), lambda b, i: (b, i, 0, 0)),
    )
    return pl.pallas_call(
        functools.partial(_mixer_in_kernel, tk=tk),
        grid=(B, nt),
        in_specs=[tok(D)] + [full(a) for a in consts]
        + [pl.BlockSpec((tm, LANES), lambda b, i: (i, 0))] * 2,
        out_specs=out_specs,
        out_shape=out_shape,
        compiler_params=_params(("parallel", "parallel")),
        name="mixer_in",
    )(x, *consts, cos, sin)


def _na_kernel(guard_ref, qT_ref, qb_ref, k0_ref, k1_ref, k2_ref, k3_ref, v0_ref, v1_ref, v2_ref, v3_ref, t_ref,
               o_ref):
    tq = qT_ref.shape[3]
    k_refs = (k0_ref, k1_ref, k2_ref, k3_ref)
    v_refs = (v0_ref, v1_ref, v2_ref, v3_ref)
    tile = (pl.program_id(0) * pl.num_programs(1) + pl.program_id(1)) * pl.num_programs(2) + pl.program_id(2)
    safe = guard_ref[tile] <= 2.0 * SAFE_BOUND

    def finish(hh, l8, acc):
        o_ref[0, hh * NA_HEAD_DIM:(hh + 1) * NA_HEAD_DIM, :] = acc / jnp.sum(l8, axis=0, keepdims=True)

    def accumulate(p, v, l8, acc):
        return l8 + jnp.sum(p.reshape(NA_CK // 8, 8, tq), axis=0), acc + _dot(v, p.astype(BF16))

    @pl.when(safe)
    def _():
        for hh in range(NA_HPS):
            qT = qT_ref[0, hh]
            shift = qb_ref[0, hh, 0:1, :]
            l8 = jnp.zeros((8, tq), F32)
            acc = jnp.zeros((NA_HEAD_DIM, tq), F32)
            for j, k_ref in enumerate(k_refs):
                p = jnp.exp2(_dot(k_ref[0, hh // 2], qT) + t_ref[0, hh, j] - shift)
                l8, acc = accumulate(p, v_refs[j][0, hh, 0], l8, acc)
            finish(hh, l8, acc)

    @pl.when(jnp.logical_not(safe))
    def _():
        for hh in range(NA_HPS):
            qT = qT_ref[0, hh]
            scores = [_dot(k_ref[0, hh // 2], qT) + t_ref[0, hh, j] for j, k_ref in enumerate(k_refs)]
            m = jnp.max(jnp.maximum(jnp.maximum(scores[0], scores[1]), jnp.maximum(scores[2], scores[3])),
                        axis=0, keepdims=True)
            l8 = jnp.zeros((8, tq), F32)
            acc = jnp.zeros((NA_HEAD_DIM, tq), F32)
            for j, sc in enumerate(scores):
                l8, acc = accumulate(jnp.exp2(sc - m), v_refs[j][0, hh, 0], l8, acc)
            finish(hh, l8, acc)


def _na_attention(guard, qT, qb, k, vT, table):
    B, H, _, S = qT.shape
    G = S // NA_TQ
    nck = S // NA_CK
    per = NA_TQ // NA_CK

    def chunk(g, j):
        return jnp.clip(g * per - 1 + j, 0, nck - 1)

    kspec = lambda j: pl.BlockSpec((1, NA_HPS // 2, NA_CK, LANES), lambda b, hp, g: (b, hp, chunk(g, j), 0))
    vspec = lambda j: pl.BlockSpec((1, NA_HPS, 1, NA_HEAD_DIM, NA_CK), lambda b, hp, g: (b, hp, chunk(g, j), 0, 0))
    variant = lambda g: jnp.where(g == 0, 0, jnp.where(g == G - 1, 2, 1))
    return pl.pallas_call(
        _na_kernel,
        grid=(B, H // NA_HPS, G),
        in_specs=[pl.BlockSpec(memory_space=pltpu.SMEM),
                  pl.BlockSpec((1, NA_HPS, LANES, NA_TQ), lambda b, hp, g: (b, hp, 0, g)),
                  pl.BlockSpec((1, NA_HPS, 8, NA_TQ), lambda b, hp, g: (b, hp, 0, g))]
        + [kspec(j) for j in range(4)] + [vspec(j) for j in range(4)]
        + [pl.BlockSpec((1, NA_HPS) + table.shape[2:], lambda b, hp, g: (variant(g), hp, 0, 0, 0))],
        out_specs=pl.BlockSpec((1, NA_HPS * NA_HEAD_DIM, NA_TQ), lambda b, hp, g: (b, hp, g)),
        out_shape=jax.ShapeDtypeStruct((B, H * NA_HEAD_DIM, S), F32),
        compiler_params=_params(("parallel", "parallel", "parallel")),
        name="na_attn",
    )(guard.reshape(-1), qT, qb, k, k, k, k, vT, vT, vT, vT, table)


def _na_bias_table(rpb, n_rows):
    G = n_rows // NA_WIN_H
    col = jnp.arange(GRID_W)
    c0 = jnp.clip(col - NA_WIN_W // 2, 0, GRID_W - NA_WIN_W)
    inwin = (col[:, None] >= c0[None, :]) & (col[:, None] < c0[None, :] + NA_WIN_W)
    dc = jnp.clip(col[:, None] - col[None, :] + (NA_WIN_W - 1), 0, 2 * NA_WIN_W - 2)
    shifted = (rpb - jnp.max(rpb, axis=(1, 2), keepdims=True)) * LOG2E
    b2 = jnp.where(inwin[None, None], shifted[:, :, dc], NEG)
    masked = jnp.full_like(b2[:, 0], NEG)

    def block(g, j, kk, e):
        kr = NA_WIN_H * g - 4 + 4 * j + kk
        qr = NA_WIN_H * g + e
        start = min(max(qr - NA_WIN_H // 2, 0), n_rows - NA_WIN_H)
        if 0 <= kr < n_rows and start <= kr < start + NA_WIN_H:
            return b2[:, kr - qr + (NA_WIN_H - 1)]
        return masked

    return jnp.stack([
        jnp.stack([
            jnp.concatenate([
                jnp.concatenate([block(g, j, kk, e) for e in range(NA_WIN_H)], axis=-1)
                for kk in range(4)], axis=1)
            for j in range(4)], axis=1)
        for g in (0, 1, G - 1)], axis=0).astype(F32)


def _mla_kernel(bound_ref, qT_ref, k_ref, vT_ref, o_ref, *, tk, unroll):
    qa = qT_ref[0, 0]
    tq = qa.shape[1]
    nk = k_ref.shape[2] // tk
    tile = (pl.program_id(0) * pl.num_programs(1) + pl.program_id(1)) * pl.num_programs(2) + pl.program_id(2)
    safe = bound_ref[tile] <= SAFE_BOUND

    def scores(c):
        off = pl.multiple_of(c * tk, tk)
        return _dot(k_ref[0, 0, pl.ds(off, tk), :], qa)

    @pl.when(safe)
    def _():
        def body(t, carry):
            l8, acc = carry
            c0 = t * unroll
            s = scores(c0)
            for u in range(unroll):
                s_next = scores(c0 + u + 1) if u + 1 < unroll else None
                p = jnp.exp2(s)
                l8 = l8 + jnp.sum(p.reshape(tk // 8, 8, tq), axis=0)
                acc = acc + _dot(vT_ref[0, 0, c0 + u], p.astype(BF16))
                s = s_next
            return l8, acc

        init = (jnp.zeros((8, tq), F32), jnp.zeros((MLA_V, tq), F32))
        l8, acc = lax.fori_loop(0, nk // unroll, body, init)
        o_ref[0] = acc / jnp.sum(l8, axis=0, keepdims=True)

    @pl.when(jnp.logical_not(safe))
    def _():
        def body(c, carry):
            m, l, acc = carry
            s = scores(c)
            m_new = jnp.maximum(m, jnp.max(s, axis=0, keepdims=True))
            alpha = jnp.exp2(m - m_new)
            p = jnp.exp2(s - m_new)
            l = alpha * l + jnp.sum(p, axis=0, keepdims=True)
            acc = alpha * acc + _dot(vT_ref[0, 0, c], p.astype(BF16))
            return m_new, l, acc

        init = (jnp.full((1, tq), NEG, F32), jnp.zeros((1, tq), F32), jnp.zeros((MLA_V, tq), F32))
        m, l, acc = lax.fori_loop(0, nk, body, init)
        o_ref[0] = acc / l


def _mla_attention(tile_bound, qT, k, vT, *, unroll):
    B, H, _, S = qT.shape
    tq = S // tile_bound.shape[2]
    tk = vT.shape[-1]
    return pl.pallas_call(
        functools.partial(_mla_kernel, tk=tk, unroll=unroll),
        grid=(B, H, S // tq),
        in_specs=[
            pl.BlockSpec(memory_space=pltpu.SMEM),
            pl.BlockSpec((1, 1, LANES, tq), lambda b, h, i: (b, h, 0, i)),
            pl.BlockSpec((1, 1, S, LANES), lambda b, h, i: (b, h, 0, 0)),
            pl.BlockSpec((1, 1, S // tk, MLA_V, tk), lambda b, h, i: (b, h, 0, 0, 0)),
        ],
        out_specs=pl.BlockSpec((1, MLA_V, tq), lambda b, h, i: (b, h, i)),
        out_shape=jax.ShapeDtypeStruct((B, H * MLA_V, S), F32),
        compiler_params=_params(("parallel", "parallel", "parallel")),
        name="mla_attn",
    )(tile_bound.reshape(-1), qT, k, vT)


def _mem_kv_kernel(mem_ref, g_ref, wkv_ref, gk_ref, k_ref, v_ref):
    mn = _rms(mem_ref[0], g_ref[...]).astype(BF16)
    kv = _dot(mn, wkv_ref[...])
    gk = gk_ref[...]
    for h in range(MEM_HEADS):
        sl = slice(h * LANES, (h + 1) * LANES)
        k_ref[0, :, sl] = _rms(kv[:, sl], gk).astype(BF16)
    v_ref[0] = kv[:, MEM_WIDTH:].astype(BF16)


def _mem_kv(mem, g, wkv, gk):
    B, M, D = mem.shape
    full = lambda a: pl.BlockSpec(a.shape, lambda b: (0,) * a.ndim)
    out = jax.ShapeDtypeStruct((B, M, MEM_WIDTH), BF16)
    ospec = pl.BlockSpec((1, M, MEM_WIDTH), lambda b: (b, 0, 0))
    return pl.pallas_call(
        _mem_kv_kernel,
        grid=(B,),
        in_specs=[pl.BlockSpec((1, M, D), lambda b: (b, 0, 0)), full(g), full(wkv), full(gk)],
        out_specs=(ospec, ospec),
        out_shape=(out, out),
        compiler_params=_params(("parallel",)),
        name="mem_kv",
    )(mem, g, wkv, gk)


def _post_mixer_kernel(x_ref, aT_ref, bT_ref, ga_ref, gb_ref, wout_ref, gmem_ref, wq_ref, gmq_ref,
                       km_ref, vm_ref, wo_ref, o_ref):
    x = x_ref[0]
    an = _rms(aT_ref[0].T, ga_ref[...]).astype(BF16)
    bn = _rms(bT_ref[0].T, gb_ref[...]).astype(BF16)
    x1 = x + _dot(an, wout_ref[:NA_WIDTH, :]) + _dot(bn, wout_ref[NA_WIDTH:, :])

    hq = _rms(x1, gmem_ref[...]).astype(BF16)
    q = _dot(hq, wq_ref[...])
    gq = gmq_ref[...] * (MEM_HEAD_DIM ** -0.5)
    outs = []
    for h in range(MEM_HEADS):
        sl = slice(h * LANES, (h + 1) * LANES)
        qn = _rms(q[:, sl], gq).astype(BF16)
        s = _dot_nt(qn, km_ref[0, :, sl])
        m = jnp.max(s, axis=-1, keepdims=True)
        p = jnp.exp(s - m)
        l = jnp.sum(p, axis=-1, keepdims=True)
        outs.append((_dot(p.astype(BF16), vm_ref[0, :, sl]) / l).astype(BF16))
    o_ref[0] = x1 + _dot(jnp.concatenate(outs, axis=-1), wo_ref[...])


def _post_mixer(x, aT, bT, ga, gb, wout, gmem, wq, gmq, km, vm, wo, *, tm):
    B, S, D = x.shape
    M = km.shape[1]
    full = lambda arr: pl.BlockSpec(arr.shape, lambda b, i: (0,) * arr.ndim)
    tok = lambda w: pl.BlockSpec((1, tm, w), lambda b, i: (b, i, 0))
    memspec = pl.BlockSpec((1, M, MEM_WIDTH), lambda b, i: (b, 0, 0))
    return pl.pallas_call(
        _post_mixer_kernel,
        grid=(B, S // tm),
        in_specs=[tok(D), pl.BlockSpec((1, NA_WIDTH, tm), lambda b, i: (b, 0, i)),
                  pl.BlockSpec((1, MLA_WIDTH, tm), lambda b, i: (b, 0, i)),
                  full(ga), full(gb), full(wout), full(gmem), full(wq), full(gmq), memspec, memspec, full(wo)],
        out_specs=tok(D),
        out_shape=jax.ShapeDtypeStruct((B, S, D), F32),
        compiler_params=_params(("parallel", "parallel")),
        name="post_mixer",
    )(x, aT, bT, ga, gb, wout, gmem, wq, gmq, km, vm, wo)


HALO = 8


def _ffn_kernel(x_ref, xp_ref, xn_ref, g_ref, wg_ref, wv_ref, cwg_ref, cwv_ref, cbg_ref, cbv_ref, wd_ref,
                o_ref, hn_ref, hh_ref, acc_ref, *, tiles_per_seq):
    i = pl.program_id(0)
    j = pl.program_id(1)
    tm = x_ref.shape[0]
    tn = wg_ref.shape[1]

    @pl.when(j == 0)
    def _():
        g = g_ref[...]
        hn_ref[...] = _rms(x_ref[...], g).astype(BF16)
        hh_ref[:HALO, :] = _rms(xp_ref[...], g).astype(BF16)
        hh_ref[HALO:, :] = _rms(xn_ref[...], g).astype(BF16)
        acc_ref[...] = jnp.zeros_like(acc_ref)

    hn = hn_ref[...]
    hh = hh_ref[...]
    has_prev = (i % tiles_per_seq != 0).astype(F32)
    has_next = ((i + 1) % tiles_per_seq != 0).astype(F32)
    row = lax.broadcasted_iota(jnp.int32, (tm, tn), 0)

    def conv(w_ref, cw_ref, cb_ref):
        w = w_ref[...]
        u = _dot(hn, w)
        uh = _dot(hh, w)
        prev = uh[HALO - 1:HALO, :] * has_prev
        nxt = uh[HALO:HALO + 1, :] * has_next
        um = jnp.where(row == 0, prev, pltpu.roll(u, 1, 0))
        up = jnp.where(row == tm - 1, nxt, pltpu.roll(u, tm - 1, 0))
        return um * cw_ref[0:1, :] + u * cw_ref[1:2, :] + up * cw_ref[2:3, :] + cb_ref[...]

    gate = conv(wg_ref, cwg_ref, cbg_ref)
    val = conv(wv_ref, cwv_ref, cbv_ref)
    act = gate / (1.0 + jnp.exp(-gate)) * val
    acc_ref[...] += _dot(act.astype(BF16), wd_ref[...])

    @pl.when(j == pl.num_programs(1) - 1)
    def _():
        o_ref[...] = x_ref[...] + acc_ref[...]


def _conv_ffn(x, g, wup, cw, cb, wd, *, tm, tn, seq):
    T, D = x.shape
    nj = D_FF // tn
    nhalo = T // HALO
    per = tm // HALO
    return pl.pallas_call(
        functools.partial(_ffn_kernel, tiles_per_seq=seq // tm),
        grid=(T // tm, nj),
        in_specs=[
            pl.BlockSpec((tm, D), lambda i, j: (i, 0)),
            pl.BlockSpec((HALO, D), lambda i, j: (jnp.maximum(i * per - 1, 0), 0)),
            pl.BlockSpec((HALO, D), lambda i, j: (jnp.minimum((i + 1) * per, nhalo - 1), 0)),
            pl.BlockSpec(g.shape, lambda i, j: (0, 0)),
            pl.BlockSpec((D, tn), lambda i, j: (0, j)),
            pl.BlockSpec((D, tn), lambda i, j: (0, nj + j)),
            pl.BlockSpec((3, tn), lambda i, j: (0, j)),
            pl.BlockSpec((3, tn), lambda i, j: (0, nj + j)),
            pl.BlockSpec((1, tn), lambda i, j: (0, j)),
            pl.BlockSpec((1, tn), lambda i, j: (0, nj + j)),
            pl.BlockSpec((tn, D), lambda i, j: (j, 0)),
        ],
        out_specs=pl.BlockSpec((tm, D), lambda i, j: (i, 0)),
        out_shape=jax.ShapeDtypeStruct((T, D), F32),
        scratch_shapes=[pltpu.VMEM((tm, D), BF16), pltpu.VMEM((2 * HALO, D), BF16), pltpu.VMEM((tm, D), F32)],
        compiler_params=_params(("parallel", "arbitrary")),
        name="conv_ffn",
    )(x, x, x, g, wup, wup, cw, cw, cb, cb, wd)


def _head_pad(w, heads, width):
    k = w.shape[0]
    w = w.reshape(k, heads, width)
    return jnp.pad(w, ((0, 0), (0, 0), (0, LANES - width))).reshape(k, heads * LANES)


def _rope_swap(a):
    half = a.shape[-1] // 2
    return jnp.concatenate([a[..., half:], a[..., :half]], axis=-1)


def _rope_sign():
    half = MLA_ROPE // 2
    return jnp.concatenate([-jnp.ones((half,), F32), jnp.ones((half,), F32)])


def _rope_tables_padded(seq):
    t = jnp.arange(seq)
    row = (t // GRID_W).astype(F32)
    col = (t % GRID_W).astype(F32)
    nfreq = MLA_ROPE // 4
    inv = 1.0 / (ROPE_BASE ** (jnp.arange(nfreq, dtype=F32) / nfreq))
    ang = jnp.concatenate([row[:, None] * inv, col[:, None] * inv], axis=-1)
    cos = jnp.concatenate([jnp.cos(ang)] * 2, axis=-1)
    sin = jnp.concatenate([jnp.sin(ang)] * 2, axis=-1)
    pad = LANES - MLA_QK
    cos = jnp.concatenate([jnp.ones((seq, MLA_NOPE), F32), cos, jnp.zeros((seq, pad), F32)], axis=-1)
    sin = jnp.concatenate([jnp.zeros((seq, MLA_NOPE), F32), sin, jnp.zeros((seq, pad), F32)], axis=-1)
    return cos, sin


def _layer_params(l, n_rows, w_in, na_q_g, na_k_g, na_rpb, q_lat_g, kv_lat_g, w_uq, w_ukv, mla_q_g, mla_k_g):
    o1, o2, o3 = NA_WIDTH, 2 * NA_WIDTH, 3 * NA_WIDTH
    o4 = o3 + MLA_Q_RANK
    o5 = o4 + MLA_KV_RANK
    win = w_in[l]
    sign = _rope_sign()
    w_kr = win[:, o5:]
    pad_rope = lambda w: jnp.pad(w, ((0, 0), (MLA_NOPE, LANES - MLA_QK)))
    wlat = jnp.concatenate([win[:, o3:o5], pad_rope(w_kr), pad_rope(_rope_swap(w_kr) * sign)], axis=-1)

    wq3 = w_uq[l].reshape(MLA_Q_RANK, MLA_HEADS, MLA_QK)
    wq3s = jnp.concatenate([jnp.zeros_like(wq3[..., :MLA_NOPE]), _rope_swap(wq3[..., MLA_NOPE:]) * sign], axis=-1)
    wkv3 = w_ukv[l].reshape(MLA_KV_RANK, MLA_HEADS, MLA_NOPE + MLA_V)

    def gain_pair(g):
        gs = jnp.concatenate([jnp.zeros((MLA_NOPE,), F32), _rope_swap(g[MLA_NOPE:])])
        padl = lambda a: jnp.pad(a, (0, LANES - MLA_QK))[None, :]
        return padl(g), padl(gs)

    gq, gqs = gain_pair(mla_q_g[l])
    gk, gks = gain_pair(mla_k_g[l])
    return dict(
        wna=win[:, :o3].astype(BF16),
        wlat=wlat.astype(BF16),
        wuq=_head_pad(wq3.reshape(MLA_Q_RANK, -1), MLA_HEADS, MLA_QK).astype(BF16),
        wuqs=_head_pad(wq3s.reshape(MLA_Q_RANK, -1), MLA_HEADS, MLA_QK).astype(BF16),
        wuk=_head_pad(wkv3[..., :MLA_NOPE].reshape(MLA_KV_RANK, -1), MLA_HEADS, MLA_NOPE).astype(BF16),
        wuv=wkv3[..., MLA_NOPE:].reshape(MLA_KV_RANK, -1).astype(BF16),
        gnaq=jnp.tile(na_q_g[l], NA_HEADS)[None, :],
        gnak=jnp.tile(na_k_g[l], NA_HEADS)[None, :],
        gql=q_lat_g[l][None, :],
        gkvl=kv_lat_g[l][None, :],
        gq=gq, gqs=gqs, gk=gk, gks=gks,
        bias=_na_bias_table(na_rpb[l], n_rows),
    )


def _tiles(S):
    tm = min(512, S)
    return dict(tm=tm, tk=min(512, tm), mla_unroll=S // min(512, tm), post_tm=min(1024, S), ffn_tm=min(1024, S),
                ffn_tn=1408)


def kernel(x, mem, mix_norm_g, w_in, na_q_g, na_k_g, na_rpb, q_lat_g, kv_lat_g, w_uq, w_ukv, mla_q_g, mla_k_g,
           grp_out_g, w_out, mem_norm_g, mem_tok_norm_g, mem_w_q, mem_w_kv, mem_q_g, mem_k_g, mem_w_o,
           ffn_norm_g, ffn_w_up, ffn_conv_w, ffn_conv_b, ffn_w_down):
    B, S, D = x.shape
    depth = w_in.shape[0]
    t = _tiles(S)
    assert D == D_MODEL and S >= 2 * NA_TQ and t["tm"] == NA_TQ and S % t["tm"] == 0 and S % t["ffn_tm"] == 0
    cos, sin = _rope_tables_padded(S)
    row = lambda a: a[None, :]
    for l in range(depth):
        p = _layer_params(l, S // GRID_W, w_in, na_q_g, na_k_g, na_rpb, q_lat_g, kv_lat_g, w_uq, w_ukv, mla_q_g, mla_k_g)
        qnT, qnb, kn, vnT, qT, k, vT, qbound = _mixer_in(
            x, row(mix_norm_g[l]), p["wna"], p["wlat"], p["wuq"], p["wuqs"], p["wuk"], p["wuv"],
            p["gnaq"], p["gnak"], p["gql"], p["gkvl"], p["gq"], p["gqs"], p["gk"], p["gks"], cos, sin,
            tm=t["tm"], tk=t["tk"])
        tb = qbound[..., 0]
        rpb_range = (jnp.max(na_rpb[l], axis=(1, 2)) - jnp.min(na_rpb[l], axis=(1, 2))) * LOG2E
        na_guard = (2.0 * tb[:, :, MLA_HEADS:] + rpb_range).reshape(B, -1, NA_HEADS // NA_HPS, NA_HPS).max(axis=-1)
        out_aT = _na_attention(na_guard.transpose(0, 2, 1), qnT, qnb, kn, vnT, p["bias"])
        out_bT = _mla_attention(tb[:, :, :MLA_HEADS].transpose(0, 2, 1), qT, k, vT, unroll=t["mla_unroll"])
        km, vm = _mem_kv(mem, row(mem_tok_norm_g[l]), mem_w_kv[l].astype(BF16), row(mem_k_g[l]))
        x = _post_mixer(
            x, out_aT, out_bT, row(grp_out_g[l, :NA_WIDTH]), row(grp_out_g[l, NA_WIDTH:]), w_out[l].astype(BF16),
            row(mem_norm_g[l]), mem_w_q[l].astype(BF16), row(mem_q_g[l]), km, vm, mem_w_o[l].astype(BF16),
            tm=t["post_tm"])
        x = _conv_ffn(
            x.reshape(B * S, D), row(ffn_norm_g[l]), ffn_w_up[l].astype(BF16), ffn_conv_w[l], row(ffn_conv_b[l]),
            ffn_w_down[l].astype(BF16), tm=t["ffn_tm"], tn=t["ffn_tn"], seq=S).reshape(B, S, D)
    return x
```

```python
import functools

import jax
import jax.numpy as jnp
from jax import lax
from jax.experimental import pallas as pl
from jax.experimental.pallas import tpu as pltpu

F32 = jnp.float32
BF16 = jnp.bfloat16

D_MODEL = 1024
GRID_W = 64
EPS = 1e-6
NA_HEADS = 8
NA_HEAD_DIM = 64
NA_WIN_H = 8
NA_WIN_W = 16
NA_WIDTH = NA_HEADS * NA_HEAD_DIM
MLA_HEADS = 8
MLA_Q_RANK = 384
MLA_KV_RANK = 256
MLA_NOPE = 64
MLA_ROPE = 32
MLA_V = 64
MLA_QK = MLA_NOPE + MLA_ROPE
MLA_WIDTH = MLA_HEADS * MLA_V
ROPE_BASE = 10000.0
MEM_HEADS = 4
MEM_HEAD_DIM = 128
MEM_WIDTH = MEM_HEADS * MEM_HEAD_DIM
D_FF = 2816
LANES = 128
NA_TQ = NA_WIN_H * GRID_W
NA_CK = 4 * GRID_W
NA_HPS = 4
NEG = -1e30
LOG2E = 1.4426950408889634
MLA_AUG = MLA_QK
SAFE_BOUND = 50.0

VMEM_LIMIT = 48 * 1024 * 1024


def _params(sem):
    return pltpu.CompilerParams(dimension_semantics=sem, vmem_limit_bytes=VMEM_LIMIT)


def _rms(x, g):
    return x * lax.rsqrt(jnp.mean(x * x, axis=-1, keepdims=True) + EPS) * g


def _dot(a, b):
    return jnp.dot(a, b, preferred_element_type=F32)


def _dot_nt(a, b):
    return lax.dot_general(a, b, (((1,), (1,)), ((), ())), preferred_element_type=F32)


def _mixer_in_kernel(x_ref, gmix_ref, wna_ref, wlat_ref, wuq_ref, wuqs_ref, wuk_ref, wuv_ref,
                     gnaq_ref, gnak_ref, gql_ref, gkvl_ref, gq_ref, gqs_ref, gk_ref, gks_ref,
                     cos_ref, sin_ref,
                     qnT_ref, qnb_ref, kn_ref, vnT_ref, qT_ref, k_ref, vT_ref, qb_ref, *, tk):
    x = x_ref[0]
    tm = x.shape[0]
    hn = _rms(x, gmix_ref[...]).astype(BF16)

    zna = _dot(hn, wna_ref[...])
    lo = lax.broadcasted_iota(jnp.int32, (tm, LANES), 1) < NA_HEAD_DIM

    def seg_norm(z, g_ref, gscale):
        blocks = []
        for c in range(NA_WIDTH // LANES):
            sl = slice(c * LANES, (c + 1) * LANES)
            blk = z[:, sl]
            sq = blk * blk
            s_lo = jnp.sum(jnp.where(lo, sq, 0.0), axis=-1, keepdims=True)
            s_hi = jnp.sum(jnp.where(lo, 0.0, sq), axis=-1, keepdims=True)
            ms = jnp.where(lo, s_lo, s_hi) * (1.0 / NA_HEAD_DIM)
            blocks.append(blk * lax.rsqrt(ms + EPS) * (g_ref[:, sl] * gscale))
        return blocks

    qn = seg_norm(zna[:, :NA_WIDTH], gnaq_ref, NA_HEAD_DIM ** -0.5 * LOG2E)
    kn = seg_norm(zna[:, NA_WIDTH:2 * NA_WIDTH], gnak_ref, 1.0)
    kmax_na = (NA_HEAD_DIM ** 0.5) * jnp.max(jnp.abs(gnak_ref[...]), axis=-1, keepdims=True)
    na_tile_bounds = []
    for c in range(NA_WIDTH // LANES):
        for sub, qm in enumerate((jnp.where(lo, qn[c], 0.0), jnp.where(lo, 0.0, qn[c]))):
            qmT = qm.T
            bound = jnp.sqrt(jnp.sum(qmT * qmT, axis=0, keepdims=True)) * kmax_na
            qnT_ref[0, 2 * c + sub] = qmT.astype(BF16)
            qnb_ref[0, 2 * c + sub] = jnp.broadcast_to(bound, (8, tm))
            na_tile_bounds.append(jnp.broadcast_to(jnp.max(bound, axis=-1, keepdims=True), (1, LANES)))
        kn_ref[0, c] = kn[c].astype(BF16)
    vnT = zna[:, 2 * NA_WIDTH:].T.astype(BF16)
    for h in range(NA_HEADS):
        for c in range(tm // NA_CK):
            vnT_ref[0, h, c] = vnT[h * NA_HEAD_DIM:(h + 1) * NA_HEAD_DIM, c * NA_CK:(c + 1) * NA_CK]

    zlat = _dot(hn, wlat_ref[...])
    o1 = MLA_Q_RANK
    o2 = o1 + MLA_KV_RANK
    cqn = _rms(zlat[:, :o1], gql_ref[...]).astype(BF16)
    ckvn = _rms(zlat[:, o1:o2], gkvl_ref[...]).astype(BF16)
    kr = zlat[:, o2:o2 + LANES]
    krs = zlat[:, o2 + LANES:o2 + 2 * LANES]
    qpre = _dot(cqn, wuq_ref[...])
    qsw = _dot(cqn, wuqs_ref[...])
    kpre = _dot(ckvn, wuk_ref[...])
    v = _dot(ckvn, wuv_ref[...])
    cosv = cos_ref[...]
    sinv = sin_ref[...]
    gq = gq_ref[...]
    gqs = gqs_ref[...]
    gk = gk_ref[...]
    gks = gks_ref[...]
    qscale = MLA_QK ** -0.5 * LOG2E
    aug = lax.broadcasted_iota(jnp.int32, (tm, LANES), 1) == MLA_AUG
    kmax = (MLA_QK ** 0.5) * jnp.max(jnp.abs(gk), axis=-1, keepdims=True)
    tile_bounds = []
    for h in range(MLA_HEADS):
        sl = slice(h * LANES, (h + 1) * LANES)
        qb = qpre[:, sl]
        rinv = lax.rsqrt(jnp.sum(qb * qb, axis=-1, keepdims=True) * (1.0 / MLA_QK) + EPS)
        qr = ((qb * gq) * cosv + (qsw[:, sl] * gqs) * sinv) * (rinv * qscale)
        bound = jnp.sqrt(jnp.sum(qr * qr, axis=-1, keepdims=True)) * kmax
        qT_ref[0, h] = jnp.where(aug, -bound, qr).T.astype(BF16)
        tile_bounds.append(jnp.broadcast_to(jnp.max(bound, axis=0, keepdims=True), (1, LANES)))
        kb = kpre[:, sl] + kr
        rinv = lax.rsqrt(jnp.sum(kb * kb, axis=-1, keepdims=True) * (1.0 / MLA_QK) + EPS)
        kk = ((kb * gk) * cosv + (krs * gks) * sinv) * rinv
        k_ref[0, h] = jnp.where(aug, 1.0, kk).astype(BF16)
    qb_ref[0, 0] = jnp.concatenate(tile_bounds + na_tile_bounds, axis=0)
    vT = v.T.astype(BF16)
    for h in range(MLA_HEADS):
        for c in range(tm // tk):
            vT_ref[0, h, c] = vT[h * MLA_V:(h + 1) * MLA_V, c * tk:(c + 1) * tk]


def _mixer_in(x, gmix, wna, wlat, wuq, wuqs, wuk, wuv, gnaq, gnak, gql, gkvl, gq, gqs, gk, gks, cos, sin, *, tm, tk):
    B, S, D = x.shape
    nt = S // tm
    full = lambda a: pl.BlockSpec(a.shape, lambda b, i: (0,) * a.ndim)
    consts = (gmix, wna, wlat, wuq, wuqs, wuk, wuv, gnaq, gnak, gql, gkvl, gq, gqs, gk, gks)
    tok = lambda w: pl.BlockSpec((1, tm, w), lambda b, i: (b, i, 0))
    out_shape = (
        jax.ShapeDtypeStruct((B, NA_HEADS, LANES, S), BF16),
        jax.ShapeDtypeStruct((B, NA_HEADS, 8, S), F32),
        jax.ShapeDtypeStruct((B, NA_WIDTH // LANES, S, LANES), BF16),
        jax.ShapeDtypeStruct((B, NA_HEADS, S // NA_CK, NA_HEAD_DIM, NA_CK), BF16),
        jax.ShapeDtypeStruct((B, MLA_HEADS, LANES, S), BF16),
        jax.ShapeDtypeStruct((B, MLA_HEADS, S, LANES), BF16),
        jax.ShapeDtypeStruct((B, MLA_HEADS, S // tk, MLA_V, tk), BF16),
        jax.ShapeDtypeStruct((B, nt, MLA_HEADS + NA_HEADS, LANES), F32),
    )
    out_specs = (
        pl.BlockSpec((1, NA_HEADS, LANES, tm), lambda b, i: (b, 0, 0, i)),
        pl.BlockSpec((1, NA_HEADS, 8, tm), lambda b, i: (b, 0, 0, i)),
        pl.BlockSpec((1, NA_WIDTH // LANES, tm, LANES), lambda b, i: (b, 0, i, 0)),
        pl.BlockSpec((1, NA_HEADS, tm // NA_CK, NA_HEAD_DIM, NA_CK), lambda b, i: (b, 0, i, 0, 0)),
        pl.BlockSpec((1, MLA_HEADS, LANES, tm), lambda b, i: (b, 0, 0, i)),
        pl.BlockSpec((1, MLA_HEADS, tm, LANES), lambda b, i: (b, 0, i, 0)),
        pl.BlockSpec((1, MLA_HEADS, tm // tk, MLA_V, tk), lambda b, i: (b, 0, i, 0, 0)),
        pl.BlockSpec((1, 1, MLA_HEADS + NA_HEADS, LANES), lambda b, i: (b, i, 0, 0)),
    )
    return pl.pallas_call(
        functools.partial(_mixer_in_kernel, tk=tk),
        grid=(B, nt),
        in_specs=[tok(D)] + [full(a) for a in consts]
        + [pl.BlockSpec((tm, LANES), lambda b, i: (i, 0))] * 2,
        out_specs=out_specs,
        out_shape=out_shape,
        compiler_params=_params(("parallel", "parallel")),
        name="mixer_in",
    )(x, *consts, cos, sin)


def _na_kernel(guard_ref, qT_ref, qb_ref, k0_ref, k1_ref, k2_ref, k3_ref, v0_ref, v1_ref, v2_ref, v3_ref, t_ref,
               o_ref):
    tq = qT_ref.shape[3]
    k_refs = (k0_ref, k1_ref, k2_ref, k3_ref)
    v_refs = (v0_ref, v1_ref, v2_ref, v3_ref)
    tile = (pl.program_id(0) * pl.num_programs(1) + pl.program_id(1)) * pl.num_programs(2) + pl.program_id(2)
    safe = guard_ref[tile] <= 2.0 * SAFE_BOUND

    def finish(hh, l8, acc):
        o_ref[0, hh * NA_HEAD_DIM:(hh + 1) * NA_HEAD_DIM, :] = acc / jnp.sum(l8, axis=0, keepdims=True)

    def accumulate(p, v, l8, acc):
        return l8 + jnp.sum(p.reshape(NA_CK // 8, 8, tq), axis=0), acc + _dot(v, p.astype(BF16))

    @pl.when(safe)
    def _():
        for hh in range(NA_HPS):
            qT = qT_ref[0, hh]
            shift = qb_ref[0, hh, 0:1, :]
            l8 = jnp.zeros((8, tq), F32)
            acc = jnp.zeros((NA_HEAD_DIM, tq), F32)
            for j, k_ref in enumerate(k_refs):
                p = jnp.exp2(_dot(k_ref[0, hh // 2], qT) + t_ref[0, hh, j] - shift)
                l8, acc = accumulate(p, v_refs[j][0, hh, 0], l8, acc)
            finish(hh, l8, acc)

    @pl.when(jnp.logical_not(safe))
    def _():
        for hh in range(NA_HPS):
            qT = qT_ref[0, hh]
            scores = [_dot(k_ref[0, hh // 2], qT) + t_ref[0, hh, j] for j, k_ref in enumerate(k_refs)]
            m = jnp.max(jnp.maximum(jnp.maximum(scores[0], scores[1]), jnp.maximum(scores[2], scores[3])),
                        axis=0, keepdims=True)
            l8 = jnp.zeros((8, tq), F32)
            acc = jnp.zeros((NA_HEAD_DIM, tq), F32)
            for j, sc in enumerate(scores):
                l8, acc = accumulate(jnp.exp2(sc - m), v_refs[j][0, hh, 0], l8, acc)
            finish(hh, l8, acc)


def _na_attention(guard, qT, qb, k, vT, table):
    B, H, _, S = qT.shape
    G = S // NA_TQ
    nck = S // NA_CK
    per = NA_TQ // NA_CK

    def chunk(g, j):
        return jnp.clip(g * per - 1 + j, 0, nck - 1)

    kspec = lambda j: pl.BlockSpec((1, NA_HPS // 2, NA_CK, LANES), lambda b, hp, g: (b, hp, chunk(g, j), 0))
    vspec = lambda j: pl.BlockSpec((1, NA_HPS, 1, NA_HEAD_DIM, NA_CK), lambda b, hp, g: (b, hp, chunk(g, j), 0, 0))
    variant = lambda g: jnp.where(g == 0, 0, jnp.where(g == G - 1, 2, 1))
    return pl.pallas_call(
        _na_kernel,
        grid=(B, H // NA_HPS, G),
        in_specs=[pl.BlockSpec(memory_space=pltpu.SMEM),
                  pl.BlockSpec((1, NA_HPS, LANES, NA_TQ), lambda b, hp, g: (b, hp, 0, g)),
                  pl.BlockSpec((1, NA_HPS, 8, NA_TQ), lambda b, hp, g: (b, hp, 0, g))]
        + [kspec(j) for j in range(4)] + [vspec(j) for j in range(4)]
        + [pl.BlockSpec((1, NA_HPS) + table.shape[2:], lambda b, hp, g: (variant(g), hp, 0, 0, 0))],
        out_specs=pl.BlockSpec((1, NA_HPS * NA_HEAD_DIM, NA_TQ), lambda b, hp, g: (b, hp, g)),
        out_shape=jax.ShapeDtypeStruct((B, H * NA_HEAD_DIM, S), F32),
        compiler_params=_params(("parallel", "parallel", "parallel")),
        name="na_attn",
    )(guard.reshape(-1), qT, qb, k, k, k, k, vT, vT, vT, vT, table)


def _na_table_kernel(b2_ref, o_ref, *, n_rows):
    G = n_rows // NA_WIN_H
    masked = b2_ref.shape[1] - 1

    def block(g, j, kk, e):
        kr = NA_WIN_H * g - 4 + 4 * j + kk
        qr = NA_WIN_H * g + e
        start = min(max(qr - NA_WIN_H // 2, 0), n_rows - NA_WIN_H)
        if 0 <= kr < n_rows and start <= kr < start + NA_WIN_H:
            return kr - qr + (NA_WIN_H - 1)
        return masked

    for v, g in enumerate((0, 1, G - 1)):
        for j in range(4):
            for kk in range(4):
                o_ref[v, 0, j, kk * GRID_W:(kk + 1) * GRID_W, :] = jnp.concatenate(
                    [b2_ref[0, block(g, j, kk, e)] for e in range(NA_WIN_H)], axis=-1)


def _na_bias_table(rpb, n_rows):
    H = rpb.shape[0]
    col = jnp.arange(GRID_W)
    c0 = jnp.clip(col - NA_WIN_W // 2, 0, GRID_W - NA_WIN_W)
    inwin = (col[:, None] >= c0[None, :]) & (col[:, None] < c0[None, :] + NA_WIN_W)
    dc = jnp.clip(col[:, None] - col[None, :] + (NA_WIN_W - 1), 0, 2 * NA_WIN_W - 2)
    shifted = (rpb - jnp.max(rpb, axis=(1, 2), keepdims=True)) * LOG2E
    b2 = jnp.where(inwin[None, None], shifted[:, :, dc], NEG)
    b2 = jnp.concatenate([b2, jnp.full_like(b2[:, :1], NEG)], axis=1).astype(F32)
    return pl.pallas_call(
        functools.partial(_na_table_kernel, n_rows=n_rows),
        grid=(H,),
        in_specs=[pl.BlockSpec((1,) + b2.shape[1:], lambda h: (h, 0, 0, 0))],
        out_specs=pl.BlockSpec((3, 1, 4, NA_CK, NA_TQ), lambda h: (0, h, 0, 0, 0)),
        out_shape=jax.ShapeDtypeStruct((3, H, 4, NA_CK, NA_TQ), F32),
        compiler_params=_params(("parallel",)),
        name="na_table",
    )(b2)


def _mla_kernel(bound_ref, qT_ref, k_ref, vT_ref, o_ref, *, tk, unroll):
    qa = qT_ref[0, 0]
    tq = qa.shape[1]
    nk = k_ref.shape[2] // tk
    tile = (pl.program_id(0) * pl.num_programs(1) + pl.program_id(1)) * pl.num_programs(2) + pl.program_id(2)
    safe = bound_ref[tile] <= SAFE_BOUND

    def scores(c):
        off = pl.multiple_of(c * tk, tk)
        return _dot(k_ref[0, 0, pl.ds(off, tk), :], qa)

    @pl.when(safe)
    def _():
        def body(t, carry):
            l8, acc = carry
            c0 = t * unroll
            s = scores(c0)
            for u in range(unroll):
                s_next = scores(c0 + u + 1) if u + 1 < unroll else None
                p = jnp.exp2(s)
                l8 = l8 + jnp.sum(p.reshape(tk // 8, 8, tq), axis=0)
                acc = acc + _dot(vT_ref[0, 0, c0 + u], p.astype(BF16))
                s = s_next
            return l8, acc

        init = (jnp.zeros((8, tq), F32), jnp.zeros((MLA_V, tq), F32))
        l8, acc = lax.fori_loop(0, nk // unroll, body, init)
        o_ref[0] = acc / jnp.sum(l8, axis=0, keepdims=True)

    @pl.when(jnp.logical_not(safe))
    def _():
        def body(c, carry):
            m, l, acc = carry
            s = scores(c)
            m_new = jnp.maximum(m, jnp.max(s, axis=0, keepdims=True))
            alpha = jnp.exp2(m - m_new)
            p = jnp.exp2(s - m_new)
            l = alpha * l + jnp.sum(p, axis=0, keepdims=True)
            acc = alpha * acc + _dot(vT_ref[0, 0, c], p.astype(BF16))
            return m_new, l, acc

        init = (jnp.full((1, tq), NEG, F32), jnp.zeros((1, tq), F32), jnp.zeros((MLA_V, tq), F32))
        m, l, acc = lax.fori_loop(0, nk, body, init)
        o_ref[0] = acc / l


def _mla_attention(tile_bound, qT, k, vT, *, unroll):
    B, H, _, S = qT.shape
    tq = S // tile_bound.shape[2]
    tk = vT.shape[-1]
    return pl.pallas_call(
        functools.partial(_mla_kernel, tk=tk, unroll=unroll),
        grid=(B, H, S // tq),
        in_specs=[
            pl.BlockSpec(memory_space=pltpu.SMEM),
            pl.BlockSpec((1, 1, LANES, tq), lambda b, h, i: (b, h, 0, i)),
            pl.BlockSpec((1, 1, S, LANES), lambda b, h, i: (b, h, 0, 0)),
            pl.BlockSpec((1, 1, S // tk, MLA_V, tk), lambda b, h, i: (b, h, 0, 0, 0)),
        ],
        out_specs=pl.BlockSpec((1, MLA_V, tq), lambda b, h, i: (b, h, i)),
        out_shape=jax.ShapeDtypeStruct((B, H * MLA_V, S), F32),
        compiler_params=_params(("parallel", "parallel", "parallel")),
        name="mla_attn",
    )(tile_bound.reshape(-1), qT, k, vT)


def _mem_kv_kernel(mem_ref, g_ref, wkv_ref, gk_ref, k_ref, v_ref):
    mn = _rms(mem_ref[0], g_ref[...]).astype(BF16)
    kv = _dot(mn, wkv_ref[...])
    gk = gk_ref[...]
    for h in range(MEM_HEADS):
        sl = slice(h * LANES, (h + 1) * LANES)
        k_ref[0, :, sl] = _rms(kv[:, sl], gk).astype(BF16)
    v_ref[0] = kv[:, MEM_WIDTH:].astype(BF16)


def _mem_kv(mem, g, wkv, gk):
    B, M, D = mem.shape
    full = lambda a: pl.BlockSpec(a.shape, lambda b: (0,) * a.ndim)
    out = jax.ShapeDtypeStruct((B, M, MEM_WIDTH), BF16)
    ospec = pl.BlockSpec((1, M, MEM_WIDTH), lambda b: (b, 0, 0))
    return pl.pallas_call(
        _mem_kv_kernel,
        grid=(B,),
        in_specs=[pl.BlockSpec((1, M, D), lambda b: (b, 0, 0)), full(g), full(wkv), full(gk)],
        out_specs=(ospec, ospec),
        out_shape=(out, out),
        compiler_params=_params(("parallel",)),
        name="mem_kv",
    )(mem, g, wkv, gk)


def _post_mixer_kernel(x_ref, aT_ref, bT_ref, ga_ref, gb_ref, wout_ref, gmem_ref, wq_ref, gmq_ref,
                       km_ref, vm_ref, wo_ref, o_ref):
    x = x_ref[0]
    an = _rms(aT_ref[0].T, ga_ref[...]).astype(BF16)
    bn = _rms(bT_ref[0].T, gb_ref[...]).astype(BF16)
    x1 = x + _dot(an, wout_ref[:NA_WIDTH, :]) + _dot(bn, wout_ref[NA_WIDTH:, :])

    hq = _rms(x1, gmem_ref[...]).astype(BF16)
    q = _dot(hq, wq_ref[...])
    gq = gmq_ref[...] * (MEM_HEAD_DIM ** -0.5)
    outs = []
    for h in range(MEM_HEADS):
        sl = slice(h * LANES, (h + 1) * LANES)
        qn = _rms(q[:, sl], gq).astype(BF16)
        s = _dot_nt(qn, km_ref[0, :, sl])
        m = jnp.max(s, axis=-1, keepdims=True)
        p = jnp.exp(s - m)
        l = jnp.sum(p, axis=-1, keepdims=True)
        outs.append((_dot(p.astype(BF16), vm_ref[0, :, sl]) / l).astype(BF16))
    o_ref[0] = x1 + _dot(jnp.concatenate(outs, axis=-1), wo_ref[...])


def _post_mixer(x, aT, bT, ga, gb, wout, gmem, wq, gmq, km, vm, wo, *, tm):
    B, S, D = x.shape
    M = km.shape[1]
    full = lambda arr: pl.BlockSpec(arr.shape, lambda b, i: (0,) * arr.ndim)
    tok = lambda w: pl.BlockSpec((1, tm, w), lambda b, i: (b, i, 0))
    memspec = pl.BlockSpec((1, M, MEM_WIDTH), lambda b, i: (b, 0, 0))
    return pl.pallas_call(
        _post_mixer_kernel,
        grid=(B, S // tm),
        in_specs=[tok(D), pl.BlockSpec((1, NA_WIDTH, tm), lambda b, i: (b, 0, i)),
                  pl.BlockSpec((1, MLA_WIDTH, tm), lambda b, i: (b, 0, i)),
                  full(ga), full(gb), full(wout), full(gmem), full(wq), full(gmq), memspec, memspec, full(wo)],
        out_specs=tok(D),
        out_shape=jax.ShapeDtypeStruct((B, S, D), F32),
        compiler_params=_params(("parallel", "parallel")),
        name="post_mixer",
    )(x, aT, bT, ga, gb, wout, gmem, wq, gmq, km, vm, wo)


HALO = 8


def _ffn_kernel(x_ref, xp_ref, xn_ref, g_ref, wg_ref, wv_ref, cwg_ref, cwv_ref, cbg_ref, cbv_ref, wd_ref,
                o_ref, hn_ref, hh_ref, acc_ref, *, tiles_per_seq):
    i = pl.program_id(0)
    j = pl.program_id(1)
    tm = x_ref.shape[0]
    tn = wg_ref.shape[1]

    @pl.when(j == 0)
    def _():
        g = g_ref[...]
        hn_ref[...] = _rms(x_ref[...], g).astype(BF16)
        hh_ref[:HALO, :] = _rms(xp_ref[...], g).astype(BF16)
        hh_ref[HALO:, :] = _rms(xn_ref[...], g).astype(BF16)
        acc_ref[...] = jnp.zeros_like(acc_ref)

    hn = hn_ref[...]
    hh = hh_ref[...]
    has_prev = (i % tiles_per_seq != 0).astype(F32)
    has_next = ((i + 1) % tiles_per_seq != 0).astype(F32)
    row = lax.broadcasted_iota(jnp.int32, (tm, tn), 0)

    def conv(w_ref, cw_ref, cb_ref):
        w = w_ref[...]
        u = _dot(hn, w)
        uh = _dot(hh, w)
        prev = uh[HALO - 1:HALO, :] * has_prev
        nxt = uh[HALO:HALO + 1, :] * has_next
        um = jnp.where(row == 0, prev, pltpu.roll(u, 1, 0))
        up = jnp.where(row == tm - 1, nxt, pltpu.roll(u, tm - 1, 0))
        return um * cw_ref[0:1, :] + u * cw_ref[1:2, :] + up * cw_ref[2:3, :] + cb_ref[...]

    gate = conv(wg_ref, cwg_ref, cbg_ref)
    val = conv(wv_ref, cwv_ref, cbv_ref)
    act = gate / (1.0 + jnp.exp(-gate)) * val
    acc_ref[...] += _dot(act.astype(BF16), wd_ref[...])

    @pl.when(j == pl.num_programs(1) - 1)
    def _():
        o_ref[...] = x_ref[...] + acc_ref[...]


def _conv_ffn(x, g, wup, cw, cb, wd, *, tm, tn, seq):
    T, D = x.shape
    nj = D_FF // tn
    nhalo = T // HALO
    per = tm // HALO
    return pl.pallas_call(
        functools.partial(_ffn_kernel, tiles_per_seq=seq // tm),
        grid=(T // tm, nj),
        in_specs=[
            pl.BlockSpec((tm, D), lambda i, j: (i, 0)),
            pl.BlockSpec((HALO, D), lambda i, j: (jnp.maximum(i * per - 1, 0), 0)),
            pl.BlockSpec((HALO, D), lambda i, j: (jnp.minimum((i + 1) * per, nhalo - 1), 0)),
            pl.BlockSpec(g.shape, lambda i, j: (0, 0)),
            pl.BlockSpec((D, tn), lambda i, j: (0, j)),
            pl.BlockSpec((D, tn), lambda i, j: (0, nj + j)),
            pl.BlockSpec((3, tn), lambda i, j: (0, j)),
            pl.BlockSpec((3, tn), lambda i, j: (0, nj + j)),
            pl.BlockSpec((1, tn), lambda i, j: (0, j)),
            pl.BlockSpec((1, tn), lambda i, j: (0, nj + j)),
            pl.BlockSpec((tn, D), lambda i, j: (j, 0)),
        ],
        out_specs=pl.BlockSpec((tm, D), lambda i, j: (i, 0)),
        out_shape=jax.ShapeDtypeStruct((T, D), F32),
        scratch_shapes=[pltpu.VMEM((tm, D), BF16), pltpu.VMEM((2 * HALO, D), BF16), pltpu.VMEM((tm, D), F32)],
        compiler_params=_params(("parallel", "arbitrary")),
        name="conv_ffn",
    )(x, x, x, g, wup, wup, cw, cw, cb, cb, wd)


def _head_pad(w, heads, width):
    k = w.shape[0]
    w = w.reshape(k, heads, width)
    return jnp.pad(w, ((0, 0), (0, 0), (0, LANES - width))).reshape(k, heads * LANES)


def _rope_swap(a):
    half = a.shape[-1] // 2
    return jnp.concatenate([a[..., half:], a[..., :half]], axis=-1)


def _rope_sign():
    half = MLA_ROPE // 2
    return jnp.concatenate([-jnp.ones((half,), F32), jnp.ones((half,), F32)])


def _rope_tables_padded(seq):
    t = jnp.arange(seq)
    row = (t // GRID_W).astype(F32)
    col = (t % GRID_W).astype(F32)
    nfreq = MLA_ROPE // 4
    inv = 1.0 / (ROPE_BASE ** (jnp.arange(nfreq, dtype=F32) / nfreq))
    ang = jnp.concatenate([row[:, None] * inv, col[:, None] * inv], axis=-1)
    cos = jnp.concatenate([jnp.cos(ang)] * 2, axis=-1)
    sin = jnp.concatenate([jnp.sin(ang)] * 2, axis=-1)
    pad = LANES - MLA_QK
    cos = jnp.concatenate([jnp.ones((seq, MLA_NOPE), F32), cos, jnp.zeros((seq, pad), F32)], axis=-1)
    sin = jnp.concatenate([jnp.zeros((seq, MLA_NOPE), F32), sin, jnp.zeros((seq, pad), F32)], axis=-1)
    return cos, sin


def _layer_params(l, n_rows, w_in, na_q_g, na_k_g, na_rpb, q_lat_g, kv_lat_g, w_uq, w_ukv, mla_q_g, mla_k_g):
    o1, o2, o3 = NA_WIDTH, 2 * NA_WIDTH, 3 * NA_WIDTH
    o4 = o3 + MLA_Q_RANK
    o5 = o4 + MLA_KV_RANK
    win = w_in[l]
    sign = _rope_sign()
    w_kr = win[:, o5:]
    pad_rope = lambda w: jnp.pad(w, ((0, 0), (MLA_NOPE, LANES - MLA_QK)))
    wlat = jnp.concatenate([win[:, o3:o5], pad_rope(w_kr), pad_rope(_rope_swap(w_kr) * sign)], axis=-1)

    wq3 = w_uq[l].reshape(MLA_Q_RANK, MLA_HEADS, MLA_QK)
    wq3s = jnp.concatenate([jnp.zeros_like(wq3[..., :MLA_NOPE]), _rope_swap(wq3[..., MLA_NOPE:]) * sign], axis=-1)
    wkv3 = w_ukv[l].reshape(MLA_KV_RANK, MLA_HEADS, MLA_NOPE + MLA_V)

    def gain_pair(g):
        gs = jnp.concatenate([jnp.zeros((MLA_NOPE,), F32), _rope_swap(g[MLA_NOPE:])])
        padl = lambda a: jnp.pad(a, (0, LANES - MLA_QK))[None, :]
        return padl(g), padl(gs)

    gq, gqs = gain_pair(mla_q_g[l])
    gk, gks = gain_pair(mla_k_g[l])
    return dict(
        wna=win[:, :o3].astype(BF16),
        wlat=wlat.astype(BF16),
        wuq=_head_pad(wq3.reshape(MLA_Q_RANK, -1), MLA_HEADS, MLA_QK).astype(BF16),
        wuqs=_head_pad(wq3s.reshape(MLA_Q_RANK, -1), MLA_HEADS, MLA_QK).astype(BF16),
        wuk=_head_pad(wkv3[..., :MLA_NOPE].reshape(MLA_KV_RANK, -1), MLA_HEADS, MLA_NOPE).astype(BF16),
        wuv=wkv3[..., MLA_NOPE:].reshape(MLA_KV_RANK, -1).astype(BF16),
        gnaq=jnp.tile(na_q_g[l], NA_HEADS)[None, :],
        gnak=jnp.tile(na_k_g[l], NA_HEADS)[None, :],
        gql=q_lat_g[l][None, :],
        gkvl=kv_lat_g[l][None, :],
        gq=gq, gqs=gqs, gk=gk, gks=gks,
        bias=_na_bias_table(na_rpb[l], n_rows),
    )


def _tiles(S):
    tm = min(512, S)
    return dict(tm=tm, tk=min(512, tm), mla_unroll=S // min(512, tm), post_tm=min(1024, S), ffn_tm=min(1024, S),
                ffn_tn=1408)


def kernel(x, mem, mix_norm_g, w_in, na_q_g, na_k_g, na_rpb, q_lat_g, kv_lat_g, w_uq, w_ukv, mla_q_g, mla_k_g,
           grp_out_g, w_out, mem_norm_g, mem_tok_norm_g, mem_w_q, mem_w_kv, mem_q_g, mem_k_g, mem_w_o,
           ffn_norm_g, ffn_w_up, ffn_conv_w, ffn_conv_b, ffn_w_down):
    B, S, D = x.shape
    depth = w_in.shape[0]
    t = _tiles(S)
    assert D == D_MODEL and S >= 2 * NA_TQ and t["tm"] == NA_TQ and S % t["tm"] == 0 and S % t["ffn_tm"] == 0
    cos, sin = _rope_tables_padded(S)
    row = lambda a: a[None, :]
    for l in range(depth):
        p = _layer_params(l, S // GRID_W, w_in, na_q_g, na_k_g, na_rpb, q_lat_g, kv_lat_g, w_uq, w_ukv, mla_q_g, mla_k_g)
        qnT, qnb, kn, vnT, qT, k, vT, qbound = _mixer_in(
            x, row(mix_norm_g[l]), p["wna"], p["wlat"], p["wuq"], p["wuqs"], p["wuk"], p["wuv"],
            p["gnaq"], p["gnak"], p["gql"], p["gkvl"], p["gq"], p["gqs"], p["gk"], p["gks"], cos, sin,
            tm=t["tm"], tk=t["tk"])
        tb = qbound[..., 0]
        rpb_range = (jnp.max(na_rpb[l], axis=(1, 2)) - jnp.min(na_rpb[l], axis=(1, 2))) * LOG2E
        na_guard = (2.0 * tb[:, :, MLA_HEADS:] + rpb_range).reshape(B, -1, NA_HEADS // NA_HPS, NA_HPS).max(axis=-1)
        out_aT = _na_attention(na_guard.transpose(0, 2, 1), qnT, qnb, kn, vnT, p["bias"])
        out_bT = _mla_attention(tb[:, :, :MLA_HEADS].transpose(0, 2, 1), qT, k, vT, unroll=t["mla_unroll"])
        km, vm = _mem_kv(mem, row(mem_tok_norm_g[l]), mem_w_kv[l].astype(BF16), row(mem_k_g[l]))
        x = _post_mixer(
            x, out_aT, out_bT, row(grp_out_g[l, :NA_WIDTH]), row(grp_out_g[l, NA_WIDTH:]), w_out[l].astype(BF16),
            row(mem_norm_g[l]), mem_w_q[l].astype(BF16), row(mem_q_g[l]), km, vm, mem_w_o[l].astype(BF16),
            tm=t["post_tm"])
        x = _conv_ffn(
            x.reshape(B * S, D), row(ffn_norm_g[l]), ffn_w_up[l].astype(BF16), ffn_conv_w[l], row(ffn_conv_b[l]),
            ffn_w_down[l].astype(BF16), tm=t["ffn_tm"], tn=t["ffn_tn"], seq=S).reshape(B, S, D)
    return x
```

```python
import functools

import jax
import jax.numpy as jnp
from jax import lax
from jax.experimental import pallas as pl
from jax.experimental.pallas import tpu as pltpu

F32 = jnp.float32
BF16 = jnp.bfloat16

D_MODEL = 1024
GRID_W = 64
EPS = 1e-6
NA_HEADS = 8
NA_HEAD_DIM = 64
NA_WIN_H = 8
NA_WIN_W = 16
NA_WIDTH = NA_HEADS * NA_HEAD_DIM
MLA_HEADS = 8
MLA_Q_RANK = 384
MLA_KV_RANK = 256
MLA_NOPE = 64
MLA_ROPE = 32
MLA_V = 64
MLA_QK = MLA_NOPE + MLA_ROPE
MLA_WIDTH = MLA_HEADS * MLA_V
ROPE_BASE = 10000.0
MEM_HEADS = 4
MEM_HEAD_DIM = 128
MEM_WIDTH = MEM_HEADS * MEM_HEAD_DIM
D_FF = 2816
LANES = 128
NA_TQ = NA_WIN_H * GRID_W
NA_CK = 4 * GRID_W
NA_HPS = 4
NEG = -1e30
LOG2E = 1.4426950408889634
MLA_AUG = MLA_QK
SAFE_BOUND = 50.0

VMEM_LIMIT = 48 * 1024 * 1024


def _params(sem):
    return pltpu.CompilerParams(dimension_semantics=sem, vmem_limit_bytes=VMEM_LIMIT)


def _rms(x, g):
    return x * lax.rsqrt(jnp.mean(x * x, axis=-1, keepdims=True) + EPS) * g


def _dot(a, b):
    return jnp.dot(a, b, preferred_element_type=F32)


def _dot_nt(a, b):
    return lax.dot_general(a, b, (((1,), (1,)), ((), ())), preferred_element_type=F32)


def _mixer_in_kernel(x_ref, gmix_ref, wna_ref, wlat_ref, wuq_ref, wuqs_ref, wuk_ref, wuv_ref,
                     gnaq_ref, gnak_ref, gql_ref, gkvl_ref, gq_ref, gqs_ref, gk_ref, gks_ref,
                     cos_ref, sin_ref,
                     qnT_ref, qnb_ref, kn_ref, vnT_ref, qT_ref, k_ref, vT_ref, qb_ref, *, tk):
    x = x_ref[0]
    tm = x.shape[0]
    hn = _rms(x, gmix_ref[...]).astype(BF16)

    zna = _dot(hn, wna_ref[...])
    lo = lax.broadcasted_iota(jnp.int32, (tm, LANES), 1) < NA_HEAD_DIM

    def seg_norm(z, g_ref, gscale):
        blocks = []
        for c in range(NA_WIDTH // LANES):
            sl = slice(c * LANES, (c + 1) * LANES)
            blk = z[:, sl]
            sq = blk * blk
            s_lo = jnp.sum(jnp.where(lo, sq, 0.0), axis=-1, keepdims=True)
            s_hi = jnp.sum(jnp.where(lo, 0.0, sq), axis=-1, keepdims=True)
            ms = jnp.where(lo, s_lo, s_hi) * (1.0 / NA_HEAD_DIM)
            blocks.append(blk * lax.rsqrt(ms + EPS) * (g_ref[:, sl] * gscale))
        return blocks

    qn = seg_norm(zna[:, :NA_WIDTH], gnaq_ref, NA_HEAD_DIM ** -0.5 * LOG2E)
    kn = seg_norm(zna[:, NA_WIDTH:2 * NA_WIDTH], gnak_ref, 1.0)
    kmax_na = (NA_HEAD_DIM ** 0.5) * jnp.max(jnp.abs(gnak_ref[...]), axis=-1, keepdims=True)
    na_tile_bounds = []
    for c in range(NA_WIDTH // LANES):
        for sub, qm in enumerate((jnp.where(lo, qn[c], 0.0), jnp.where(lo, 0.0, qn[c]))):
            qmT = qm.T
            bound = jnp.sqrt(jnp.sum(qmT * qmT, axis=0, keepdims=True)) * kmax_na
            qnT_ref[0, 2 * c + sub] = qmT.astype(BF16)
            qnb_ref[0, 2 * c + sub] = jnp.broadcast_to(bound, (8, tm))
            na_tile_bounds.append(jnp.broadcast_to(jnp.max(bound, axis=-1, keepdims=True), (1, LANES)))
        kn_ref[0, c] = kn[c].astype(BF16)
    vnT = zna[:, 2 * NA_WIDTH:].T.astype(BF16)
    for h in range(NA_HEADS):
        for c in range(tm // NA_CK):
            vnT_ref[0, h, c] = vnT[h * NA_HEAD_DIM:(h + 1) * NA_HEAD_DIM, c * NA_CK:(c + 1) * NA_CK]

    zlat = _dot(hn, wlat_ref[...])
    o1 = MLA_Q_RANK
    o2 = o1 + MLA_KV_RANK
    cqn = _rms(zlat[:, :o1], gql_ref[...]).astype(BF16)
    ckvn = _rms(zlat[:, o1:o2], gkvl_ref[...]).astype(BF16)
    kr = zlat[:, o2:o2 + LANES]
    krs = zlat[:, o2 + LANES:o2 + 2 * LANES]
    qpre = _dot(cqn, wuq_ref[...])
    qsw = _dot(cqn, wuqs_ref[...])
    kpre = _dot(ckvn, wuk_ref[...])
    v = _dot(ckvn, wuv_ref[...])
    cosv = cos_ref[...]
    sinv = sin_ref[...]
    gq = gq_ref[...]
    gqs = gqs_ref[...]
    gk = gk_ref[...]
    gks = gks_ref[...]
    qscale = MLA_QK ** -0.5 * LOG2E
    aug = lax.broadcasted_iota(jnp.int32, (tm, LANES), 1) == MLA_AUG
    kmax = (MLA_QK ** 0.5) * jnp.max(jnp.abs(gk), axis=-1, keepdims=True)
    tile_bounds = []
    for h in range(MLA_HEADS):
        sl = slice(h * LANES, (h + 1) * LANES)
        qb = qpre[:, sl]
        rinv = lax.rsqrt(jnp.sum(qb * qb, axis=-1, keepdims=True) * (1.0 / MLA_QK) + EPS)
        qr = ((qb * gq) * cosv + (qsw[:, sl] * gqs) * sinv) * (rinv * qscale)
        bound = jnp.sqrt(jnp.sum(qr * qr, axis=-1, keepdims=True)) * kmax
        qT_ref[0, h] = jnp.where(aug, -bound, qr).T.astype(BF16)
        tile_bounds.append(jnp.broadcast_to(jnp.max(bound, axis=0, keepdims=True), (1, LANES)))
        kb = kpre[:, sl] + kr
        rinv = lax.rsqrt(jnp.sum(kb * kb, axis=-1, keepdims=True) * (1.0 / MLA_QK) + EPS)
        kk = ((kb * gk) * cosv + (krs * gks) * sinv) * rinv
        k_ref[0, h] = jnp.where(aug, 1.0, kk).astype(BF16)
    qb_ref[0, 0] = jnp.concatenate(tile_bounds + na_tile_bounds, axis=0)
    vT = v.T.astype(BF16)
    for h in range(MLA_HEADS):
        for c in range(tm // tk):
            vT_ref[0, h, c] = vT[h * MLA_V:(h + 1) * MLA_V, c * tk:(c + 1) * tk]


def _mixer_in(x, gmix, wna, wlat, wuq, wuqs, wuk, wuv, gnaq, gnak, gql, gkvl, gq, gqs, gk, gks, cos, sin, *, tm, tk):
    B, S, D = x.shape
    nt = S // tm
    full = lambda a: pl.BlockSpec(a.shape, lambda b, i: (0,) * a.ndim)
    consts = (gmix, wna, wlat, wuq, wuqs, wuk, wuv, gnaq, gnak, gql, gkvl, gq, gqs, gk, gks)
    tok = lambda w: pl.BlockSpec((1, tm, w), lambda b, i: (b, i, 0))
    out_shape = (
        jax.ShapeDtypeStruct((B, NA_HEADS, LANES, S), BF16),
        jax.ShapeDtypeStruct((B, NA_HEADS, 8, S), F32),
        jax.ShapeDtypeStruct((B, NA_WIDTH // LANES, S, LANES), BF16),
        jax.ShapeDtypeStruct((B, NA_HEADS, S // NA_CK, NA_HEAD_DIM, NA_CK), BF16),
        jax.ShapeDtypeStruct((B, MLA_HEADS, LANES, S), BF16),
        jax.ShapeDtypeStruct((B, MLA_HEADS, S, LANES), BF16),
        jax.ShapeDtypeStruct((B, MLA_HEADS, S // tk, MLA_V, tk), BF16),
        jax.ShapeDtypeStruct((B, nt, MLA_HEADS + NA_HEADS, LANES), F32),
    )
    out_specs = (
        pl.BlockSpec((1, NA_HEADS, LANES, tm), lambda b, i: (b, 0, 0, i)),
        pl.BlockSpec((1, NA_HEADS, 8, tm), lambda b, i: (b, 0, 0, i)),
        pl.BlockSpec((1, NA_WIDTH // LANES, tm, LANES), lambda b, i: (b, 0, i, 0)),
        pl.BlockSpec((1, NA_HEADS, tm // NA_CK, NA_HEAD_DIM, NA_CK), lambda b, i: (b, 0, i, 0, 0)),
        pl.BlockSpec((1, MLA_HEADS, LANES, tm), lambda b, i: (b, 0, 0, i)),
        pl.BlockSpec((1, MLA_HEADS, tm, LANES), lambda b, i: (b, 0, i, 0)),
        pl.BlockSpec((1, MLA_HEADS, tm // tk, MLA_V, tk), lambda b, i: (b, 0, i, 0, 0)),
        pl.BlockSpec((1, 1, MLA_HEADS + NA_HEADS, LANES), lambda b, i: (b, i, 0, 0)),
    )
    return pl.pallas_call(
        functools.partial(_mixer_in_kernel, tk=tk),
        grid=(B, nt),
        in_specs=[tok(D)] + [full(a) for a in consts]
        + [pl.BlockSpec((tm, LANES), lambda b, i: (i, 0))] * 2,
        out_specs=out_specs,
        out_shape=out_shape,
        compiler_params=_params(("parallel", "parallel")),
        name="mixer_in",
    )(x, *consts, cos, sin)


def _na_kernel(guard_ref, qT_ref, qb_ref, k0_ref, k1_ref, k2_ref, k3_ref, v0_ref, v1_ref, v2_ref, v3_ref, t_ref,
               o_ref):
    tq = qT_ref.shape[3]
    k_refs = (k0_ref, k1_ref, k2_ref, k3_ref)
    v_refs = (v0_ref, v1_ref, v2_ref, v3_ref)
    tile = (pl.program_id(0) * pl.num_programs(1) + pl.program_id(1)) * pl.num_programs(2) + pl.program_id(2)
    safe = guard_ref[tile] <= 2.0 * SAFE_BOUND

    def finish(hh, l8, acc):
        o_ref[0, hh * NA_HEAD_DIM:(hh + 1) * NA_HEAD_DIM, :] = acc / jnp.sum(l8, axis=0, keepdims=True)

    def accumulate(p, v, l8, acc):
        return l8 + jnp.sum(p.reshape(NA_CK // 8, 8, tq), axis=0), acc + _dot(v, p.astype(BF16))

    @pl.when(safe)
    def _():
        for hh in range(NA_HPS):
            qT = qT_ref[0, hh]
            shift = qb_ref[0, hh, 0:1, :]
            nkeys = len(k_refs) * NA_CK
            kcat = jnp.concatenate([k_ref[0, hh // 2] for k_ref in k_refs], axis=0)
            vcat = jnp.concatenate([v_ref[0, hh, 0] for v_ref in v_refs], axis=1)
            p = jnp.exp2(_dot(kcat, qT) + t_ref[0, hh].reshape(nkeys, tq) - shift)
            finish(hh, jnp.sum(p.reshape(nkeys // 8, 8, tq), axis=0), _dot(vcat, p.astype(BF16)))

    @pl.when(jnp.logical_not(safe))
    def _():
        for hh in range(NA_HPS):
            qT = qT_ref[0, hh]
            scores = [_dot(k_ref[0, hh // 2], qT) + t_ref[0, hh, j] for j, k_ref in enumerate(k_refs)]
            m = jnp.max(jnp.maximum(jnp.maximum(scores[0], scores[1]), jnp.maximum(scores[2], scores[3])),
                        axis=0, keepdims=True)
            l8 = jnp.zeros((8, tq), F32)
            acc = jnp.zeros((NA_HEAD_DIM, tq), F32)
            for j, sc in enumerate(scores):
                l8, acc = accumulate(jnp.exp2(sc - m), v_refs[j][0, hh, 0], l8, acc)
            finish(hh, l8, acc)


def _na_attention(guard, qT, qb, k, vT, table):
    B, H, _, S = qT.shape
    G = S // NA_TQ
    nck = S // NA_CK
    per = NA_TQ // NA_CK

    def chunk(g, j):
        return jnp.clip(g * per - 1 + j, 0, nck - 1)

    kspec = lambda j: pl.BlockSpec((1, NA_HPS // 2, NA_CK, LANES), lambda b, hp, g: (b, hp, chunk(g, j), 0))
    vspec = lambda j: pl.BlockSpec((1, NA_HPS, 1, NA_HEAD_DIM, NA_CK), lambda b, hp, g: (b, hp, chunk(g, j), 0, 0))
    variant = lambda g: jnp.where(g == 0, 0, jnp.where(g == G - 1, 2, 1))
    return pl.pallas_call(
        _na_kernel,
        grid=(B, H // NA_HPS, G),
        in_specs=[pl.BlockSpec(memory_space=pltpu.SMEM),
                  pl.BlockSpec((1, NA_HPS, LANES, NA_TQ), lambda b, hp, g: (b, hp, 0, g)),
                  pl.BlockSpec((1, NA_HPS, 8, NA_TQ), lambda b, hp, g: (b, hp, 0, g))]
        + [kspec(j) for j in range(4)] + [vspec(j) for j in range(4)]
        + [pl.BlockSpec((1, NA_HPS) + table.shape[2:], lambda b, hp, g: (variant(g), hp, 0, 0, 0))],
        out_specs=pl.BlockSpec((1, NA_HPS * NA_HEAD_DIM, NA_TQ), lambda b, hp, g: (b, hp, g)),
        out_shape=jax.ShapeDtypeStruct((B, H * NA_HEAD_DIM, S), F32),
        compiler_params=_params(("parallel", "parallel", "parallel")),
        name="na_attn",
    )(guard.reshape(-1), qT, qb, k, k, k, k, vT, vT, vT, vT, table)


def _na_table_kernel(b2_ref, o_ref, *, n_rows):
    G = n_rows // NA_WIN_H
    masked = b2_ref.shape[1] - 1

    def block(g, j, kk, e):
        kr = NA_WIN_H * g - 4 + 4 * j + kk
        qr = NA_WIN_H * g + e
        start = min(max(qr - NA_WIN_H // 2, 0), n_rows - NA_WIN_H)
        if 0 <= kr < n_rows and start <= kr < start + NA_WIN_H:
            return kr - qr + (NA_WIN_H - 1)
        return masked

    for v, g in enumerate((0, 1, G - 1)):
        for j in range(4):
            for kk in range(4):
                o_ref[v, 0, j, kk * GRID_W:(kk + 1) * GRID_W, :] = jnp.concatenate(
                    [b2_ref[0, block(g, j, kk, e)] for e in range(NA_WIN_H)], axis=-1)


def _na_bias_table(rpb, n_rows):
    H = rpb.shape[0]
    col = jnp.arange(GRID_W)
    c0 = jnp.clip(col - NA_WIN_W // 2, 0, GRID_W - NA_WIN_W)
    inwin = (col[:, None] >= c0[None, :]) & (col[:, None] < c0[None, :] + NA_WIN_W)
    dc = jnp.clip(col[:, None] - col[None, :] + (NA_WIN_W - 1), 0, 2 * NA_WIN_W - 2)
    shifted = (rpb - jnp.max(rpb, axis=(1, 2), keepdims=True)) * LOG2E
    b2 = jnp.where(inwin[None, None], shifted[:, :, dc], NEG)
    b2 = jnp.concatenate([b2, jnp.full_like(b2[:, :1], NEG)], axis=1).astype(F32)
    return pl.pallas_call(
        functools.partial(_na_table_kernel, n_rows=n_rows),
        grid=(H,),
        in_specs=[pl.BlockSpec((1,) + b2.shape[1:], lambda h: (h, 0, 0, 0))],
        out_specs=pl.BlockSpec((3, 1, 4, NA_CK, NA_TQ), lambda h: (0, h, 0, 0, 0)),
        out_shape=jax.ShapeDtypeStruct((3, H, 4, NA_CK, NA_TQ), F32),
        compiler_params=_params(("parallel",)),
        name="na_table",
    )(b2)


def _mla_kernel(bound_ref, qT_ref, k_ref, vT_ref, o_ref, *, tk, unroll):
    qa = qT_ref[0, 0]
    tq = qa.shape[1]
    nk = k_ref.shape[2] // tk
    tile = (pl.program_id(0) * pl.num_programs(1) + pl.program_id(1)) * pl.num_programs(2) + pl.program_id(2)
    safe = bound_ref[tile] <= SAFE_BOUND

    def scores(c):
        off = pl.multiple_of(c * tk, tk)
        return _dot(k_ref[0, 0, pl.ds(off, tk), :], qa)

    @pl.when(safe)
    def _():
        def body(t, carry):
            l8, acc = carry
            c0 = t * unroll
            s = scores(c0)
            for u in range(unroll):
                s_next = scores(c0 + u + 1) if u + 1 < unroll else None
                p = jnp.exp2(s)
                l8 = l8 + jnp.sum(p.reshape(tk // 8, 8, tq), axis=0)
                acc = acc + _dot(vT_ref[0, 0, c0 + u], p.astype(BF16))
                s = s_next
            return l8, acc

        init = (jnp.zeros((8, tq), F32), jnp.zeros((MLA_V, tq), F32))
        l8, acc = lax.fori_loop(0, nk // unroll, body, init)
        o_ref[0] = acc / jnp.sum(l8, axis=0, keepdims=True)

    @pl.when(jnp.logical_not(safe))
    def _():
        def body(c, carry):
            m, l, acc = carry
            s = scores(c)
            m_new = jnp.maximum(m, jnp.max(s, axis=0, keepdims=True))
            alpha = jnp.exp2(m - m_new)
            p = jnp.exp2(s - m_new)
            l = alpha * l + jnp.sum(p, axis=0, keepdims=True)
            acc = alpha * acc + _dot(vT_ref[0, 0, c], p.astype(BF16))
            return m_new, l, acc

        init = (jnp.full((1, tq), NEG, F32), jnp.zeros((1, tq), F32), jnp.zeros((MLA_V, tq), F32))
        m, l, acc = lax.fori_loop(0, nk, body, init)
        o_ref[0] = acc / l


def _mla_attention(tile_bound, qT, k, vT, *, unroll):
    B, H, _, S = qT.shape
    tq = S // tile_bound.shape[2]
    tk = vT.shape[-1]
    return pl.pallas_call(
        functools.partial(_mla_kernel, tk=tk, unroll=unroll),
        grid=(B, H, S // tq),
        in_specs=[
            pl.BlockSpec(memory_space=pltpu.SMEM),
            pl.BlockSpec((1, 1, LANES, tq), lambda b, h, i: (b, h, 0, i)),
            pl.BlockSpec((1, 1, S, LANES), lambda b, h, i: (b, h, 0, 0)),
            pl.BlockSpec((1, 1, S // tk, MLA_V, tk), lambda b, h, i: (b, h, 0, 0, 0)),
        ],
        out_specs=pl.BlockSpec((1, MLA_V, tq), lambda b, h, i: (b, h, i)),
        out_shape=jax.ShapeDtypeStruct((B, H * MLA_V, S), F32),
        compiler_params=_params(("parallel", "parallel", "parallel")),
        name="mla_attn",
    )(tile_bound.reshape(-1), qT, k, vT)


def _mem_kv_kernel(mem_ref, g_ref, wkv_ref, gk_ref, k_ref, v_ref):
    mn = _rms(mem_ref[0], g_ref[...]).astype(BF16)
    kv = _dot(mn, wkv_ref[...])
    gk = gk_ref[...]
    for h in range(MEM_HEADS):
        sl = slice(h * LANES, (h + 1) * LANES)
        k_ref[0, :, sl] = _rms(kv[:, sl], gk).astype(BF16)
    v_ref[0] = kv[:, MEM_WIDTH:].astype(BF16)


def _mem_kv(mem, g, wkv, gk):
    B, M, D = mem.shape
    full = lambda a: pl.BlockSpec(a.shape, lambda b: (0,) * a.ndim)
    out = jax.ShapeDtypeStruct((B, M, MEM_WIDTH), BF16)
    ospec = pl.BlockSpec((1, M, MEM_WIDTH), lambda b: (b, 0, 0))
    return pl.pallas_call(
        _mem_kv_kernel,
        grid=(B,),
        in_specs=[pl.BlockSpec((1, M, D), lambda b: (b, 0, 0)), full(g), full(wkv), full(gk)],
        out_specs=(ospec, ospec),
        out_shape=(out, out),
        compiler_params=_params(("parallel",)),
        name="mem_kv",
    )(mem, g, wkv, gk)


def _post_mixer_kernel(x_ref, aT_ref, bT_ref, ga_ref, gb_ref, wout_ref, gmem_ref, wq_ref, gmq_ref,
                       km_ref, vm_ref, wo_ref, o_ref):
    x = x_ref[0]
    an = _rms(aT_ref[0].T, ga_ref[...]).astype(BF16)
    bn = _rms(bT_ref[0].T, gb_ref[...]).astype(BF16)
    x1 = x + _dot(an, wout_ref[:NA_WIDTH, :]) + _dot(bn, wout_ref[NA_WIDTH:, :])

    hq = _rms(x1, gmem_ref[...]).astype(BF16)
    q = _dot(hq, wq_ref[...])
    gq = gmq_ref[...] * (MEM_HEAD_DIM ** -0.5)
    outs = []
    for h in range(MEM_HEADS):
        sl = slice(h * LANES, (h + 1) * LANES)
        qn = _rms(q[:, sl], gq).astype(BF16)
        s = _dot_nt(qn, km_ref[0, :, sl])
        m = jnp.max(s, axis=-1, keepdims=True)
        p = jnp.exp(s - m)
        l = jnp.sum(p, axis=-1, keepdims=True)
        outs.append((_dot(p.astype(BF16), vm_ref[0, :, sl]) / l).astype(BF16))
    o_ref[0] = x1 + _dot(jnp.concatenate(outs, axis=-1), wo_ref[...])


def _post_mixer(x, aT, bT, ga, gb, wout, gmem, wq, gmq, km, vm, wo, *, tm):
    B, S, D = x.shape
    M = km.shape[1]
    full = lambda arr: pl.BlockSpec(arr.shape, lambda b, i: (0,) * arr.ndim)
    tok = lambda w: pl.BlockSpec((1, tm, w), lambda b, i: (b, i, 0))
    memspec = pl.BlockSpec((1, M, MEM_WIDTH), lambda b, i: (b, 0, 0))
    return pl.pallas_call(
        _post_mixer_kernel,
        grid=(B, S // tm),
        in_specs=[tok(D), pl.BlockSpec((1, NA_WIDTH, tm), lambda b, i: (b, 0, i)),
                  pl.BlockSpec((1, MLA_WIDTH, tm), lambda b, i: (b, 0, i)),
                  full(ga), full(gb), full(wout), full(gmem), full(wq), full(gmq), memspec, memspec, full(wo)],
        out_specs=tok(D),
        out_shape=jax.ShapeDtypeStruct((B, S, D), F32),
        compiler_params=_params(("parallel", "parallel")),
        name="post_mixer",
    )(x, aT, bT, ga, gb, wout, gmem, wq, gmq, km, vm, wo)


HALO = 8


def _ffn_kernel(x_ref, xp_ref, xn_ref, g_ref, wg_ref, wv_ref, cwg_ref, cwv_ref, cbg_ref, cbv_ref, wd_ref,
                o_ref, hn_ref, acc_ref, *, tiles_per_seq):
    i = pl.program_id(0)
    j = pl.program_id(1)
    tm = x_ref.shape[0]
    tn = wg_ref.shape[1]

    @pl.when(j == 0)
    def _():
        g = g_ref[...]
        hn_ref[:tm, :] = _rms(x_ref[...], g).astype(BF16)
        hn_ref[tm:tm + HALO, :] = _rms(xp_ref[...], g).astype(BF16)
        hn_ref[tm + HALO:, :] = _rms(xn_ref[...], g).astype(BF16)
        acc_ref[...] = jnp.zeros_like(acc_ref)

    hn = hn_ref[...]
    has_prev = (i % tiles_per_seq != 0).astype(F32)
    has_next = ((i + 1) % tiles_per_seq != 0).astype(F32)
    row = lax.broadcasted_iota(jnp.int32, (tm, tn), 0)

    def conv(w_ref, cw_ref, cb_ref):
        ua = _dot(hn, w_ref[...])
        u = ua[:tm]
        prev = ua[tm + HALO - 1:tm + HALO, :] * has_prev
        nxt = ua[tm + HALO:tm + HALO + 1, :] * has_next
        um = jnp.where(row == 0, prev, pltpu.roll(u, 1, 0))
        up = jnp.where(row == tm - 1, nxt, pltpu.roll(u, tm - 1, 0))
        return um * cw_ref[0:1, :] + u * cw_ref[1:2, :] + up * cw_ref[2:3, :] + cb_ref[...]

    gate = conv(wg_ref, cwg_ref, cbg_ref)
    val = conv(wv_ref, cwv_ref, cbv_ref)
    act = gate / (1.0 + jnp.exp(-gate)) * val
    acc_ref[...] += _dot(act.astype(BF16), wd_ref[...])

    @pl.when(j == pl.num_programs(1) - 1)
    def _():
        o_ref[...] = x_ref[...] + acc_ref[...]


def _conv_ffn(x, g, wup, cw, cb, wd, *, tm, tn, seq):
    T, D = x.shape
    nj = D_FF // tn
    nhalo = T // HALO
    per = tm // HALO
    return pl.pallas_call(
        functools.partial(_ffn_kernel, tiles_per_seq=seq // tm),
        grid=(T // tm, nj),
        in_specs=[
            pl.BlockSpec((tm, D), lambda i, j: (i, 0)),
            pl.BlockSpec((HALO, D), lambda i, j: (jnp.maximum(i * per - 1, 0), 0)),
            pl.BlockSpec((HALO, D), lambda i, j: (jnp.minimum((i + 1) * per, nhalo - 1), 0)),
            pl.BlockSpec(g.shape, lambda i, j: (0, 0)),
            pl.BlockSpec((D, tn), lambda i, j: (0, j)),
            pl.BlockSpec((D, tn), lambda i, j: (0, nj + j)),
            pl.BlockSpec((3, tn), lambda i, j: (0, j)),
            pl.BlockSpec((3, tn), lambda i, j: (0, nj + j)),
            pl.BlockSpec((1, tn), lambda i, j: (0, j)),
            pl.BlockSpec((1, tn), lambda i, j: (0, nj + j)),
            pl.BlockSpec((tn, D), lambda i, j: (j, 0)),
        ],
        out_specs=pl.BlockSpec((tm, D), lambda i, j: (i, 0)),
        out_shape=jax.ShapeDtypeStruct((T, D), F32),
        scratch_shapes=[pltpu.VMEM((tm + 2 * HALO, D), BF16), pltpu.VMEM((tm, D), F32)],
        compiler_params=_params(("parallel", "arbitrary")),
        name="conv_ffn",
    )(x, x, x, g, wup, wup, cw, cw, cb, cb, wd)


def _head_pad(w, heads, width):
    k = w.shape[0]
    w = w.reshape(k, heads, width)
    return jnp.pad(w, ((0, 0), (0, 0), (0, LANES - width))).reshape(k, heads * LANES)


def _rope_swap(a):
    half = a.shape[-1] // 2
    return jnp.concatenate([a[..., half:], a[..., :half]], axis=-1)


def _rope_sign():
    half = MLA_ROPE // 2
    return jnp.concatenate([-jnp.ones((half,), F32), jnp.ones((half,), F32)])


def _rope_tables_padded(seq):
    t = jnp.arange(seq)
    row = (t // GRID_W).astype(F32)
    col = (t % GRID_W).astype(F32)
    nfreq = MLA_ROPE // 4
    inv = 1.0 / (ROPE_BASE ** (jnp.arange(nfreq, dtype=F32) / nfreq))
    ang = jnp.concatenate([row[:, None] * inv, col[:, None] * inv], axis=-1)
    cos = jnp.concatenate([jnp.cos(ang)] * 2, axis=-1)
    sin = jnp.concatenate([jnp.sin(ang)] * 2, axis=-1)
    pad = LANES - MLA_QK
    cos = jnp.concatenate([jnp.ones((seq, MLA_NOPE), F32), cos, jnp.zeros((seq, pad), F32)], axis=-1)
    sin = jnp.concatenate([jnp.zeros((seq, MLA_NOPE), F32), sin, jnp.zeros((seq, pad), F32)], axis=-1)
    return cos, sin


def _layer_params(l, n_rows, w_in, na_q_g, na_k_g, na_rpb, q_lat_g, kv_lat_g, w_uq, w_ukv, mla_q_g, mla_k_g):
    o1, o2, o3 = NA_WIDTH, 2 * NA_WIDTH, 3 * NA_WIDTH
    o4 = o3 + MLA_Q_RANK
    o5 = o4 + MLA_KV_RANK
    win = w_in[l]
    sign = _rope_sign()
    w_kr = win[:, o5:]
    pad_rope = lambda w: jnp.pad(w, ((0, 0), (MLA_NOPE, LANES - MLA_QK)))
    wlat = jnp.concatenate([win[:, o3:o5], pad_rope(w_kr), pad_rope(_rope_swap(w_kr) * sign)], axis=-1)

    wq3 = w_uq[l].reshape(MLA_Q_RANK, MLA_HEADS, MLA_QK)
    wq3s = jnp.concatenate([jnp.zeros_like(wq3[..., :MLA_NOPE]), _rope_swap(wq3[..., MLA_NOPE:]) * sign], axis=-1)
    wkv3 = w_ukv[l].reshape(MLA_KV_RANK, MLA_HEADS, MLA_NOPE + MLA_V)

    def gain_pair(g):
        gs = jnp.concatenate([jnp.zeros((MLA_NOPE,), F32), _rope_swap(g[MLA_NOPE:])])
        padl = lambda a: jnp.pad(a, (0, LANES - MLA_QK))[None, :]
        return padl(g), padl(gs)

    gq, gqs = gain_pair(mla_q_g[l])
    gk, gks = gain_pair(mla_k_g[l])
    return dict(
        wna=win[:, :o3].astype(BF16),
        wlat=wlat.astype(BF16),
        wuq=_head_pad(wq3.reshape(MLA_Q_RANK, -1), MLA_HEADS, MLA_QK).astype(BF16),
        wuqs=_head_pad(wq3s.reshape(MLA_Q_RANK, -1), MLA_HEADS, MLA_QK).astype(BF16),
        wuk=_head_pad(wkv3[..., :MLA_NOPE].reshape(MLA_KV_RANK, -1), MLA_HEADS, MLA_NOPE).astype(BF16),
        wuv=wkv3[..., MLA_NOPE:].reshape(MLA_KV_RANK, -1).astype(BF16),
        gnaq=jnp.tile(na_q_g[l], NA_HEADS)[None, :],
        gnak=jnp.tile(na_k_g[l], NA_HEADS)[None, :],
        gql=q_lat_g[l][None, :],
        gkvl=kv_lat_g[l][None, :],
        gq=gq, gqs=gqs, gk=gk, gks=gks,
        bias=_na_bias_table(na_rpb[l], n_rows),
    )


def _tiles(S):
    tm = min(512, S)
    return dict(tm=tm, tk=min(512, tm), mla_unroll=S // min(512, tm), post_tm=min(1024, S), ffn_tm=min(1024, S),
                ffn_tn=1408)


def kernel(x, mem, mix_norm_g, w_in, na_q_g, na_k_g, na_rpb, q_lat_g, kv_lat_g, w_uq, w_ukv, mla_q_g, mla_k_g,
           grp_out_g, w_out, mem_norm_g, mem_tok_norm_g, mem_w_q, mem_w_kv, mem_q_g, mem_k_g, mem_w_o,
           ffn_norm_g, ffn_w_up, ffn_conv_w, ffn_conv_b, ffn_w_down):
    B, S, D = x.shape
    depth = w_in.shape[0]
    t = _tiles(S)
    assert D == D_MODEL and S >= 2 * NA_TQ and t["tm"] == NA_TQ and S % t["tm"] == 0 and S % t["ffn_tm"] == 0
    cos, sin = _rope_tables_padded(S)
    row = lambda a: a[None, :]
    for l in range(depth):
        p = _layer_params(l, S // GRID_W, w_in, na_q_g, na_k_g, na_rpb, q_lat_g, kv_lat_g, w_uq, w_ukv, mla_q_g, mla_k_g)
        qnT, qnb, kn, vnT, qT, k, vT, qbound = _mixer_in(
            x, row(mix_norm_g[l]), p["wna"], p["wlat"], p["wuq"], p["wuqs"], p["wuk"], p["wuv"],
            p["gnaq"], p["gnak"], p["gql"], p["gkvl"], p["gq"], p["gqs"], p["gk"], p["gks"], cos, sin,
            tm=t["tm"], tk=t["tk"])
        tb = qbound[..., 0]
        rpb_range = (jnp.max(na_rpb[l], axis=(1, 2)) - jnp.min(na_rpb[l], axis=(1, 2))) * LOG2E
        na_guard = (2.0 * tb[:, :, MLA_HEADS:] + rpb_range).reshape(B, -1, NA_HEADS // NA_HPS, NA_HPS).max(axis=-1)
        out_aT = _na_attention(na_guard.transpose(0, 2, 1), qnT, qnb, kn, vnT, p["bias"])
        out_bT = _mla_attention(tb[:, :, :MLA_HEADS].transpose(0, 2, 1), qT, k, vT, unroll=t["mla_unroll"])
        km, vm = _mem_kv(mem, row(mem_tok_norm_g[l]), mem_w_kv[l].astype(BF16), row(mem_k_g[l]))
        x = _post_mixer(
            x, out_aT, out_bT, row(grp_out_g[l, :NA_WIDTH]), row(grp_out_g[l, NA_WIDTH:]), w_out[l].astype(BF16),
            row(mem_norm_g[l]), mem_w_q[l].astype(BF16), row(mem_q_g[l]), km, vm, mem_w_o[l].astype(BF16),
            tm=t["post_tm"])
        x = _conv_ffn(
            x.reshape(B * S, D), row(ffn_norm_g[l]), ffn_w_up[l].astype(BF16), ffn_conv_w[l], row(ffn_conv_b[l]),
            ffn_w_down[l].astype(BF16), tm=t["ffn_tm"], tn=t["ffn_tn"], seq=S).reshape(B, S, D)
    return x
```

```python
import functools

import jax
import jax.numpy as jnp
from jax import lax
from jax.experimental import pallas as pl
from jax.experimental.pallas import tpu as pltpu

F32 = jnp.float32
BF16 = jnp.bfloat16

D_MODEL = 1024
GRID_W = 64
EPS = 1e-6
NA_HEADS = 8
NA_HEAD_DIM = 64
NA_WIN_H = 8
NA_WIN_W = 16
NA_WIDTH = NA_HEADS * NA_HEAD_DIM
MLA_HEADS = 8
MLA_Q_RANK = 384
MLA_KV_RANK = 256
MLA_NOPE = 64
MLA_ROPE = 32
MLA_V = 64
MLA_QK = MLA_NOPE + MLA_ROPE
MLA_WIDTH = MLA_HEADS * MLA_V
ROPE_BASE = 10000.0
MEM_HEADS = 4
MEM_HEAD_DIM = 128
MEM_WIDTH = MEM_HEADS * MEM_HEAD_DIM
D_FF = 2816
LANES = 128
NA_TQ = NA_WIN_H * GRID_W
NA_CK = 4 * GRID_W
NA_HPS = 8
NEG = -1e30
LOG2E = 1.4426950408889634
MLA_AUG = MLA_QK
SAFE_BOUND = 50.0

VMEM_LIMIT = 48 * 1024 * 1024


def _params(sem):
    return pltpu.CompilerParams(dimension_semantics=sem, vmem_limit_bytes=VMEM_LIMIT)


def _rms(x, g):
    return x * lax.rsqrt(jnp.mean(x * x, axis=-1, keepdims=True) + EPS) * g


def _dot(a, b):
    return jnp.dot(a, b, preferred_element_type=F32)


def _dot_nt(a, b):
    return lax.dot_general(a, b, (((1,), (1,)), ((), ())), preferred_element_type=F32)


def _mixer_in_kernel(x_ref, gmix_ref, wna_ref, wlat_ref, wuq_ref, wuqs_ref, wuk_ref, wuv_ref,
                     gnaq_ref, gnak_ref, gql_ref, gkvl_ref, gq_ref, gqs_ref, gk_ref, gks_ref,
                     cos_ref, sin_ref,
                     qnT_ref, qnb_ref, kn_ref, vnT_ref, qT_ref, k_ref, vT_ref, qb_ref, *, tk):
    x = x_ref[0]
    tm = x.shape[0]
    hn = _rms(x, gmix_ref[...]).astype(BF16)

    zna = _dot(hn, wna_ref[...])
    lo = lax.broadcasted_iota(jnp.int32, (tm, LANES), 1) < NA_HEAD_DIM

    def seg_norm(z, g_ref, gscale):
        blocks = []
        for c in range(NA_WIDTH // LANES):
            sl = slice(c * LANES, (c + 1) * LANES)
            blk = z[:, sl]
            sq = blk * blk
            s_lo = jnp.sum(jnp.where(lo, sq, 0.0), axis=-1, keepdims=True)
            s_hi = jnp.sum(jnp.where(lo, 0.0, sq), axis=-1, keepdims=True)
            ms = jnp.where(lo, s_lo, s_hi) * (1.0 / NA_HEAD_DIM)
            blocks.append(blk * lax.rsqrt(ms + EPS) * (g_ref[:, sl] * gscale))
        return blocks

    qn = seg_norm(zna[:, :NA_WIDTH], gnaq_ref, NA_HEAD_DIM ** -0.5 * LOG2E)
    kn = seg_norm(zna[:, NA_WIDTH:2 * NA_WIDTH], gnak_ref, 1.0)
    kmax_na = (NA_HEAD_DIM ** 0.5) * jnp.max(jnp.abs(gnak_ref[...]), axis=-1, keepdims=True)
    na_tile_bounds = []
    for c in range(NA_WIDTH // LANES):
        for sub, qm in enumerate((jnp.where(lo, qn[c], 0.0), jnp.where(lo, 0.0, qn[c]))):
            qmT = qm.T
            bound = jnp.sqrt(jnp.sum(qmT * qmT, axis=0, keepdims=True)) * kmax_na
            qnT_ref[0, 2 * c + sub] = qmT.astype(BF16)
            qnb_ref[0, 2 * c + sub] = jnp.broadcast_to(bound, (8, tm))
            na_tile_bounds.append(jnp.broadcast_to(jnp.max(bound, axis=-1, keepdims=True), (1, LANES)))
        kn_ref[0, c] = kn[c].astype(BF16)
    vnT = zna[:, 2 * NA_WIDTH:].T.astype(BF16)
    for h in range(NA_HEADS):
        for c in range(tm // NA_CK):
            vnT_ref[0, h, c] = vnT[h * NA_HEAD_DIM:(h + 1) * NA_HEAD_DIM, c * NA_CK:(c + 1) * NA_CK]

    zlat = _dot(hn, wlat_ref[...])
    o1 = MLA_Q_RANK
    o2 = o1 + MLA_KV_RANK
    cqn = _rms(zlat[:, :o1], gql_ref[...]).astype(BF16)
    ckvn = _rms(zlat[:, o1:o2], gkvl_ref[...]).astype(BF16)
    kr = zlat[:, o2:o2 + LANES]
    krs = zlat[:, o2 + LANES:o2 + 2 * LANES]
    qpre = _dot(cqn, wuq_ref[...])
    qsw = _dot(cqn, wuqs_ref[...])
    kpre = _dot(ckvn, wuk_ref[...])
    v = _dot(ckvn, wuv_ref[...])
    cosv = cos_ref[...]
    sinv = sin_ref[...]
    gq = gq_ref[...]
    gqs = gqs_ref[...]
    gk = gk_ref[...]
    gks = gks_ref[...]
    qscale = MLA_QK ** -0.5 * LOG2E
    aug = lax.broadcasted_iota(jnp.int32, (tm, LANES), 1) == MLA_AUG
    kmax = (MLA_QK ** 0.5) * jnp.max(jnp.abs(gk), axis=-1, keepdims=True)
    tile_bounds = []
    for h in range(MLA_HEADS):
        sl = slice(h * LANES, (h + 1) * LANES)
        qb = qpre[:, sl]
        rinv = lax.rsqrt(jnp.sum(qb * qb, axis=-1, keepdims=True) * (1.0 / MLA_QK) + EPS)
        qr = ((qb * gq) * cosv + (qsw[:, sl] * gqs) * sinv) * (rinv * qscale)
        bound = jnp.sqrt(jnp.sum(qr * qr, axis=-1, keepdims=True)) * kmax
        qT_ref[0, h] = jnp.where(aug, -bound, qr).T.astype(BF16)
        tile_bounds.append(jnp.broadcast_to(jnp.max(bound, axis=0, keepdims=True), (1, LANES)))
        kb = kpre[:, sl] + kr
        rinv = lax.rsqrt(jnp.sum(kb * kb, axis=-1, keepdims=True) * (1.0 / MLA_QK) + EPS)
        kk = ((kb * gk) * cosv + (krs * gks) * sinv) * rinv
        k_ref[0, h] = jnp.where(aug, 1.0, kk).astype(BF16)
    qb_ref[0, 0] = jnp.concatenate(tile_bounds + na_tile_bounds, axis=0)
    vT = v.T.astype(BF16)
    for h in range(MLA_HEADS):
        for c in range(tm // tk):
            vT_ref[0, h, c] = vT[h * MLA_V:(h + 1) * MLA_V, c * tk:(c + 1) * tk]


def _mixer_in(x, gmix, wna, wlat, wuq, wuqs, wuk, wuv, gnaq, gnak, gql, gkvl, gq, gqs, gk, gks, cos, sin, *, tm, tk):
    B, S, D = x.shape
    nt = S // tm
    full = lambda a: pl.BlockSpec(a.shape, lambda b, i: (0,) * a.ndim)
    consts = (gmix, wna, wlat, wuq, wuqs, wuk, wuv, gnaq, gnak, gql, gkvl, gq, gqs, gk, gks)
    tok = lambda w: pl.BlockSpec((1, tm, w), lambda b, i: (b, i, 0))
    out_shape = (
        jax.ShapeDtypeStruct((B, NA_HEADS, LANES, S), BF16),
        jax.ShapeDtypeStruct((B, NA_HEADS, 8, S), F32),
        jax.ShapeDtypeStruct((B, NA_WIDTH // LANES, S, LANES), BF16),
        jax.ShapeDtypeStruct((B, NA_HEADS, S // NA_CK, NA_HEAD_DIM, NA_CK), BF16),
        jax.ShapeDtypeStruct((B, MLA_HEADS, LANES, S), BF16),
        jax.ShapeDtypeStruct((B, MLA_HEADS, S, LANES), BF16),
        jax.ShapeDtypeStruct((B, MLA_HEADS, S // tk, MLA_V, tk), BF16),
        jax.ShapeDtypeStruct((B, nt, MLA_HEADS + NA_HEADS, LANES), F32),
    )
    out_specs = (
        pl.BlockSpec((1, NA_HEADS, LANES, tm), lambda b, i: (b, 0, 0, i)),
        pl.BlockSpec((1, NA_HEADS, 8, tm), lambda b, i: (b, 0, 0, i)),
        pl.BlockSpec((1, NA_WIDTH // LANES, tm, LANES), lambda b, i: (b, 0, i, 0)),
        pl.BlockSpec((1, NA_HEADS, tm // NA_CK, NA_HEAD_DIM, NA_CK), lambda b, i: (b, 0, i, 0, 0)),
        pl.BlockSpec((1, MLA_HEADS, LANES, tm), lambda b, i: (b, 0, 0, i)),
        pl.BlockSpec((1, MLA_HEADS, tm, LANES), lambda b, i: (b, 0, i, 0)),
        pl.BlockSpec((1, MLA_HEADS, tm // tk, MLA_V, tk), lambda b, i: (b, 0, i, 0, 0)),
        pl.BlockSpec((1, 1, MLA_HEADS + NA_HEADS, LANES), lambda b, i: (b, i, 0, 0)),
    )
    return pl.pallas_call(
        functools.partial(_mixer_in_kernel, tk=tk),
        grid=(B, nt),
        in_specs=[tok(D)] + [full(a) for a in consts]
        + [pl.BlockSpec((tm, LANES), lambda b, i: (i, 0))] * 2,
        out_specs=out_specs,
        out_shape=out_shape,
        compiler_params=_params(("parallel", "parallel")),
        name="mixer_in",
    )(x, *consts, cos, sin)


def _na_kernel(guard_ref, qT_ref, qb_ref, k0_ref, k1_ref, k2_ref, k3_ref, v0_ref, v1_ref, v2_ref, v3_ref, t_ref,
               o_ref):
    tq = qT_ref.shape[3]
    k_refs = (k0_ref, k1_ref, k2_ref, k3_ref)
    v_refs = (v0_ref, v1_ref, v2_ref, v3_ref)
    tile = (pl.program_id(0) * pl.num_programs(1) + pl.program_id(1)) * pl.num_programs(2) + pl.program_id(2)
    safe = guard_ref[tile] <= 2.0 * SAFE_BOUND

    def finish(hh, l8, acc):
        o_ref[0, hh * NA_HEAD_DIM:(hh + 1) * NA_HEAD_DIM, :] = acc / jnp.sum(l8, axis=0, keepdims=True)

    def accumulate(p, v, l8, acc):
        return l8 + jnp.sum(p.reshape(NA_CK // 8, 8, tq), axis=0), acc + _dot(v, p.astype(BF16))

    @pl.when(safe)
    def _():
        for hh in range(NA_HPS):
            qT = qT_ref[0, hh]
            shift = qb_ref[0, hh, 0:1, :]
            nkeys = len(k_refs) * NA_CK
            kcat = jnp.concatenate([k_ref[0, hh // 2] for k_ref in k_refs], axis=0)
            vcat = jnp.concatenate([v_ref[0, hh, 0] for v_ref in v_refs], axis=1)
            p = jnp.exp2(_dot(kcat, qT) + t_ref[0, hh].reshape(nkeys, tq) - shift)
            finish(hh, jnp.sum(p.reshape(nkeys // 8, 8, tq), axis=0), _dot(vcat, p.astype(BF16)))

    @pl.when(jnp.logical_not(safe))
    def _():
        for hh in range(NA_HPS):
            qT = qT_ref[0, hh]
            scores = [_dot(k_ref[0, hh // 2], qT) + t_ref[0, hh, j] for j, k_ref in enumerate(k_refs)]
            m = jnp.max(jnp.maximum(jnp.maximum(scores[0], scores[1]), jnp.maximum(scores[2], scores[3])),
                        axis=0, keepdims=True)
            l8 = jnp.zeros((8, tq), F32)
            acc = jnp.zeros((NA_HEAD_DIM, tq), F32)
            for j, sc in enumerate(scores):
                l8, acc = accumulate(jnp.exp2(sc - m), v_refs[j][0, hh, 0], l8, acc)
            finish(hh, l8, acc)


def _na_attention(guard, qT, qb, k, vT, table):
    B, H, _, S = qT.shape
    G = S // NA_TQ
    nck = S // NA_CK
    per = NA_TQ // NA_CK

    def chunk(g, j):
        return jnp.clip(g * per - 1 + j, 0, nck - 1)

    kspec = lambda j: pl.BlockSpec((1, NA_HPS // 2, NA_CK, LANES), lambda b, hp, g: (b, hp, chunk(g, j), 0))
    vspec = lambda j: pl.BlockSpec((1, NA_HPS, 1, NA_HEAD_DIM, NA_CK), lambda b, hp, g: (b, hp, chunk(g, j), 0, 0))
    variant = lambda g: jnp.where(g == 0, 0, jnp.where(g == G - 1, 2, 1))
    return pl.pallas_call(
        _na_kernel,
        grid=(B, H // NA_HPS, G),
        in_specs=[pl.BlockSpec(memory_space=pltpu.SMEM),
                  pl.BlockSpec((1, NA_HPS, LANES, NA_TQ), lambda b, hp, g: (b, hp, 0, g)),
                  pl.BlockSpec((1, NA_HPS, 8, NA_TQ), lambda b, hp, g: (b, hp, 0, g))]
        + [kspec(j) for j in range(4)] + [vspec(j) for j in range(4)]
        + [pl.BlockSpec((1, NA_HPS) + table.shape[2:], lambda b, hp, g: (variant(g), hp, 0, 0, 0))],
        out_specs=pl.BlockSpec((1, NA_HPS * NA_HEAD_DIM, NA_TQ), lambda b, hp, g: (b, hp, g)),
        out_shape=jax.ShapeDtypeStruct((B, H * NA_HEAD_DIM, S), F32),
        compiler_params=_params(("parallel", "parallel", "parallel")),
        name="na_attn",
    )(guard.reshape(-1), qT, qb, k, k, k, k, vT, vT, vT, vT, table)


def _na_table_kernel(b2_ref, o_ref, *, n_rows):
    G = n_rows // NA_WIN_H
    masked = b2_ref.shape[1] - 1

    def block(g, j, kk, e):
        kr = NA_WIN_H * g - 4 + 4 * j + kk
        qr = NA_WIN_H * g + e
        start = min(max(qr - NA_WIN_H // 2, 0), n_rows - NA_WIN_H)
        if 0 <= kr < n_rows and start <= kr < start + NA_WIN_H:
            return kr - qr + (NA_WIN_H - 1)
        return masked

    for v, g in enumerate((0, 1, G - 1)):
        for j in range(4):
            for kk in range(4):
                o_ref[v, 0, j, kk * GRID_W:(kk + 1) * GRID_W, :] = jnp.concatenate(
                    [b2_ref[0, block(g, j, kk, e)] for e in range(NA_WIN_H)], axis=-1)


def _na_bias_table(rpb, n_rows):
    H = rpb.shape[0]
    col = jnp.arange(GRID_W)
    c0 = jnp.clip(col - NA_WIN_W // 2, 0, GRID_W - NA_WIN_W)
    inwin = (col[:, None] >= c0[None, :]) & (col[:, None] < c0[None, :] + NA_WIN_W)
    dc = jnp.clip(col[:, None] - col[None, :] + (NA_WIN_W - 1), 0, 2 * NA_WIN_W - 2)
    shifted = (rpb - jnp.max(rpb, axis=(1, 2), keepdims=True)) * LOG2E
    onehot = (dc[None] == jnp.arange(2 * NA_WIN_W - 1)[:, None, None]).astype(F32)
    spread = jnp.einsum("hdk,kcq->hdcq", shifted, onehot, precision=lax.Precision.HIGHEST)
    b2 = jnp.where(inwin[None, None], spread, NEG)
    b2 = jnp.concatenate([b2, jnp.full_like(b2[:, :1], NEG)], axis=1).astype(F32)
    return pl.pallas_call(
        functools.partial(_na_table_kernel, n_rows=n_rows),
        grid=(H,),
        in_specs=[pl.BlockSpec((1,) + b2.shape[1:], lambda h: (h, 0, 0, 0))],
        out_specs=pl.BlockSpec((3, 1, 4, NA_CK, NA_TQ), lambda h: (0, h, 0, 0, 0)),
        out_shape=jax.ShapeDtypeStruct((3, H, 4, NA_CK, NA_TQ), F32),
        compiler_params=_params(("parallel",)),
        name="na_table",
    )(b2)


def _mla_kernel(bound_ref, qT_ref, k_ref, vT_ref, o_ref, *, tk, unroll):
    qa = qT_ref[0, 0]
    tq = qa.shape[1]
    nk = k_ref.shape[2] // tk
    tile = (pl.program_id(0) * pl.num_programs(1) + pl.program_id(1)) * pl.num_programs(2) + pl.program_id(2)
    safe = bound_ref[tile] <= SAFE_BOUND

    def scores(c):
        off = pl.multiple_of(c * tk, tk)
        return _dot(k_ref[0, 0, pl.ds(off, tk), :], qa)

    @pl.when(safe)
    def _():
        def body(t, carry):
            l8, acc = carry
            c0 = t * unroll
            s = scores(c0)
            for u in range(unroll):
                s_next = scores(c0 + u + 1) if u + 1 < unroll else None
                p = jnp.exp2(s)
                l8 = l8 + jnp.sum(p.reshape(tk // 8, 8, tq), axis=0)
                acc = acc + _dot(vT_ref[0, 0, c0 + u], p.astype(BF16))
                s = s_next
            return l8, acc

        init = (jnp.zeros((8, tq), F32), jnp.zeros((MLA_V, tq), F32))
        l8, acc = lax.fori_loop(0, nk // unroll, body, init)
        o_ref[0] = acc / jnp.sum(l8, axis=0, keepdims=True)

    @pl.when(jnp.logical_not(safe))
    def _():
        def body(c, carry):
            m, l, acc = carry
            s = scores(c)
            m_new = jnp.maximum(m, jnp.max(s, axis=0, keepdims=True))
            alpha = jnp.exp2(m - m_new)
            p = jnp.exp2(s - m_new)
            l = alpha * l + jnp.sum(p, axis=0, keepdims=True)
            acc = alpha * acc + _dot(vT_ref[0, 0, c], p.astype(BF16))
            return m_new, l, acc

        init = (jnp.full((1, tq), NEG, F32), jnp.zeros((1, tq), F32), jnp.zeros((MLA_V, tq), F32))
        m, l, acc = lax.fori_loop(0, nk, body, init)
        o_ref[0] = acc / l


def _mla_attention(tile_bound, qT, k, vT, *, unroll):
    B, H, _, S = qT.shape
    tq = S // tile_bound.shape[2]
    tk = vT.shape[-1]
    return pl.pallas_call(
        functools.partial(_mla_kernel, tk=tk, unroll=unroll),
        grid=(B, H, S // tq),
        in_specs=[
            pl.BlockSpec(memory_space=pltpu.SMEM),
            pl.BlockSpec((1, 1, LANES, tq), lambda b, h, i: (b, h, 0, i)),
            pl.BlockSpec((1, 1, S, LANES), lambda b, h, i: (b, h, 0, 0)),
            pl.BlockSpec((1, 1, S // tk, MLA_V, tk), lambda b, h, i: (b, h, 0, 0, 0)),
        ],
        out_specs=pl.BlockSpec((1, MLA_V, tq), lambda b, h, i: (b, h, i)),
        out_shape=jax.ShapeDtypeStruct((B, H * MLA_V, S), F32),
        compiler_params=_params(("parallel", "parallel", "parallel")),
        name="mla_attn",
    )(tile_bound.reshape(-1), qT, k, vT)


def _mem_kv_kernel(mem_ref, g_ref, wkv_ref, gk_ref, k_ref, v_ref):
    mn = _rms(mem_ref[0], g_ref[...]).astype(BF16)
    kv = _dot(mn, wkv_ref[...])
    gk = gk_ref[...]
    for h in range(MEM_HEADS):
        sl = slice(h * LANES, (h + 1) * LANES)
        k_ref[0, :, sl] = _rms(kv[:, sl], gk).astype(BF16)
    v_ref[0] = kv[:, MEM_WIDTH:].astype(BF16)


def _mem_kv(mem, g, wkv, gk):
    B, M, D = mem.shape
    full = lambda a: pl.BlockSpec(a.shape, lambda b: (0,) * a.ndim)
    out = jax.ShapeDtypeStruct((B, M, MEM_WIDTH), BF16)
    ospec = pl.BlockSpec((1, M, MEM_WIDTH), lambda b: (b, 0, 0))
    return pl.pallas_call(
        _mem_kv_kernel,
        grid=(B,),
        in_specs=[pl.BlockSpec((1, M, D), lambda b: (b, 0, 0)), full(g), full(wkv), full(gk)],
        out_specs=(ospec, ospec),
        out_shape=(out, out),
        compiler_params=_params(("parallel",)),
        name="mem_kv",
    )(mem, g, wkv, gk)


def _post_mixer_kernel(x_ref, aT_ref, bT_ref, ga_ref, gb_ref, wout_ref, gmem_ref, wq_ref, gmq_ref,
                       km_ref, vm_ref, wo_ref, o_ref):
    x = x_ref[0]
    an = _rms(aT_ref[0].T, ga_ref[...]).astype(BF16)
    bn = _rms(bT_ref[0].T, gb_ref[...]).astype(BF16)
    x1 = x + _dot(an, wout_ref[:NA_WIDTH, :]) + _dot(bn, wout_ref[NA_WIDTH:, :])

    hq = _rms(x1, gmem_ref[...]).astype(BF16)
    q = _dot(hq, wq_ref[...])
    gq = gmq_ref[...] * (MEM_HEAD_DIM ** -0.5)
    outs = []
    for h in range(MEM_HEADS):
        sl = slice(h * LANES, (h + 1) * LANES)
        qn = _rms(q[:, sl], gq).astype(BF16)
        s = _dot_nt(qn, km_ref[0, :, sl])
        m = jnp.max(s, axis=-1, keepdims=True)
        p = jnp.exp(s - m)
        l = jnp.sum(p, axis=-1, keepdims=True)
        outs.append((_dot(p.astype(BF16), vm_ref[0, :, sl]) / l).astype(BF16))
    o_ref[0] = x1 + _dot(jnp.concatenate(outs, axis=-1), wo_ref[...])


def _post_mixer(x, aT, bT, ga, gb, wout, gmem, wq, gmq, km, vm, wo, *, tm):
    B, S, D = x.shape
    M = km.shape[1]
    full = lambda arr: pl.BlockSpec(arr.shape, lambda b, i: (0,) * arr.ndim)
    tok = lambda w: pl.BlockSpec((1, tm, w), lambda b, i: (b, i, 0))
    memspec = pl.BlockSpec((1, M, MEM_WIDTH), lambda b, i: (b, 0, 0))
    return pl.pallas_call(
        _post_mixer_kernel,
        grid=(B, S // tm),
        in_specs=[tok(D), pl.BlockSpec((1, NA_WIDTH, tm), lambda b, i: (b, 0, i)),
                  pl.BlockSpec((1, MLA_WIDTH, tm), lambda b, i: (b, 0, i)),
                  full(ga), full(gb), full(wout), full(gmem), full(wq), full(gmq), memspec, memspec, full(wo)],
        out_specs=tok(D),
        out_shape=jax.ShapeDtypeStruct((B, S, D), F32),
        compiler_params=_params(("parallel", "parallel")),
        name="post_mixer",
    )(x, aT, bT, ga, gb, wout, gmem, wq, gmq, km, vm, wo)


HALO = 8


def _ffn_kernel(x_ref, xp_ref, xn_ref, g_ref, wg_ref, wv_ref, cwg_ref, cwv_ref, cbg_ref, cbv_ref, wd_ref,
                o_ref, hn_ref, acc_ref, *, tiles_per_seq):
    i = pl.program_id(0)
    j = pl.program_id(1)
    tm = x_ref.shape[0]
    tn = wg_ref.shape[1]

    @pl.when(j == 0)
    def _():
        g = g_ref[...]
        hn_ref[:tm, :] = _rms(x_ref[...], g).astype(BF16)
        hn_ref[tm:tm + HALO, :] = _rms(xp_ref[...], g).astype(BF16)
        hn_ref[tm + HALO:, :] = _rms(xn_ref[...], g).astype(BF16)
        acc_ref[...] = jnp.zeros_like(acc_ref)

    hn = hn_ref[...]
    has_prev = (i % tiles_per_seq != 0).astype(F32)
    has_next = ((i + 1) % tiles_per_seq != 0).astype(F32)
    row = lax.broadcasted_iota(jnp.int32, (tm, tn), 0)

    def conv(w_ref, cw_ref, cb_ref):
        ua = _dot(hn, w_ref[...])
        u = ua[:tm]
        prev = ua[tm + HALO - 1:tm + HALO, :] * has_prev
        nxt = ua[tm + HALO:tm + HALO + 1, :] * has_next
        um = jnp.where(row == 0, prev, pltpu.roll(u, 1, 0))
        up = jnp.where(row == tm - 1, nxt, pltpu.roll(u, tm - 1, 0))
        return um * cw_ref[0:1, :] + u * cw_ref[1:2, :] + up * cw_ref[2:3, :] + cb_ref[...]

    gate = conv(wg_ref, cwg_ref, cbg_ref)
    val = conv(wv_ref, cwv_ref, cbv_ref)
    act = gate / (1.0 + jnp.exp(-gate)) * val
    acc_ref[...] += _dot(act.astype(BF16), wd_ref[...])

    @pl.when(j == pl.num_programs(1) - 1)
    def _():
        o_ref[...] = x_ref[...] + acc_ref[...]


def _conv_ffn(x, g, wup, cw, cb, wd, *, tm, tn, seq):
    T, D = x.shape
    nj = D_FF // tn
    nhalo = T // HALO
    per = tm // HALO
    return pl.pallas_call(
        functools.partial(_ffn_kernel, tiles_per_seq=seq // tm),
        grid=(T // tm, nj),
        in_specs=[
            pl.BlockSpec((tm, D), lambda i, j: (i, 0)),
            pl.BlockSpec((HALO, D), lambda i, j: (jnp.maximum(i * per - 1, 0), 0)),
            pl.BlockSpec((HALO, D), lambda i, j: (jnp.minimum((i + 1) * per, nhalo - 1), 0)),
            pl.BlockSpec(g.shape, lambda i, j: (0, 0)),
            pl.BlockSpec((D, tn), lambda i, j: (0, j)),
            pl.BlockSpec((D, tn), lambda i, j: (0, nj + j)),
            pl.BlockSpec((3, tn), lambda i, j: (0, j)),
            pl.BlockSpec((3, tn), lambda i, j: (0, nj + j)),
            pl.BlockSpec((1, tn), lambda i, j: (0, j)),
            pl.BlockSpec((1, tn), lambda i, j: (0, nj + j)),
            pl.BlockSpec((tn, D), lambda i, j: (j, 0)),
        ],
        out_specs=pl.BlockSpec((tm, D), lambda i, j: (i, 0)),
        out_shape=jax.ShapeDtypeStruct((T, D), F32),
        scratch_shapes=[pltpu.VMEM((tm + 2 * HALO, D), BF16), pltpu.VMEM((tm, D), F32)],
        compiler_params=_params(("parallel", "arbitrary")),
        name="conv_ffn",
    )(x, x, x, g, wup, wup, cw, cw, cb, cb, wd)


def _head_pad(w, heads, width):
    k = w.shape[0]
    w = w.reshape(k, heads, width)
    return jnp.pad(w, ((0, 0), (0, 0), (0, LANES - width))).reshape(k, heads * LANES)


def _rope_swap(a):
    half = a.shape[-1] // 2
    return jnp.concatenate([a[..., half:], a[..., :half]], axis=-1)


def _rope_sign():
    half = MLA_ROPE // 2
    return jnp.concatenate([-jnp.ones((half,), F32), jnp.ones((half,), F32)])


def _rope_tables_padded(seq):
    t = jnp.arange(seq)
    row = (t // GRID_W).astype(F32)
    col = (t % GRID_W).astype(F32)
    nfreq = MLA_ROPE // 4
    inv = 1.0 / (ROPE_BASE ** (jnp.arange(nfreq, dtype=F32) / nfreq))
    ang = jnp.concatenate([row[:, None] * inv, col[:, None] * inv], axis=-1)
    cos = jnp.concatenate([jnp.cos(ang)] * 2, axis=-1)
    sin = jnp.concatenate([jnp.sin(ang)] * 2, axis=-1)
    pad = LANES - MLA_QK
    cos = jnp.concatenate([jnp.ones((seq, MLA_NOPE), F32), cos, jnp.zeros((seq, pad), F32)], axis=-1)
    sin = jnp.concatenate([jnp.zeros((seq, MLA_NOPE), F32), sin, jnp.zeros((seq, pad), F32)], axis=-1)
    return cos, sin


def _layer_params(l, n_rows, w_in, na_q_g, na_k_g, na_rpb, q_lat_g, kv_lat_g, w_uq, w_ukv, mla_q_g, mla_k_g):
    o1, o2, o3 = NA_WIDTH, 2 * NA_WIDTH, 3 * NA_WIDTH
    o4 = o3 + MLA_Q_RANK
    o5 = o4 + MLA_KV_RANK
    win = w_in[l]
    sign = _rope_sign()
    w_kr = win[:, o5:]
    pad_rope = lambda w: jnp.pad(w, ((0, 0), (MLA_NOPE, LANES - MLA_QK)))
    wlat = jnp.concatenate([win[:, o3:o5], pad_rope(w_kr), pad_rope(_rope_swap(w_kr) * sign)], axis=-1)

    wq3 = w_uq[l].reshape(MLA_Q_RANK, MLA_HEADS, MLA_QK)
    wq3s = jnp.concatenate([jnp.zeros_like(wq3[..., :MLA_NOPE]), _rope_swap(wq3[..., MLA_NOPE:]) * sign], axis=-1)
    wkv3 = w_ukv[l].reshape(MLA_KV_RANK, MLA_HEADS, MLA_NOPE + MLA_V)

    def gain_pair(g):
        gs = jnp.concatenate([jnp.zeros((MLA_NOPE,), F32), _rope_swap(g[MLA_NOPE:])])
        padl = lambda a: jnp.pad(a, (0, LANES - MLA_QK))[None, :]
        return padl(g), padl(gs)

    gq, gqs = gain_pair(mla_q_g[l])
    gk, gks = gain_pair(mla_k_g[l])
    return dict(
        wna=win[:, :o3].astype(BF16),
        wlat=wlat.astype(BF16),
        wuq=_head_pad(wq3.reshape(MLA_Q_RANK, -1), MLA_HEADS, MLA_QK).astype(BF16),
        wuqs=_head_pad(wq3s.reshape(MLA_Q_RANK, -1), MLA_HEADS, MLA_QK).astype(BF16),
        wuk=_head_pad(wkv3[..., :MLA_NOPE].reshape(MLA_KV_RANK, -1), MLA_HEADS, MLA_NOPE).astype(BF16),
        wuv=wkv3[..., MLA_NOPE:].reshape(MLA_KV_RANK, -1).astype(BF16),
        gnaq=jnp.tile(na_q_g[l], NA_HEADS)[None, :],
        gnak=jnp.tile(na_k_g[l], NA_HEADS)[None, :],
        gql=q_lat_g[l][None, :],
        gkvl=kv_lat_g[l][None, :],
        gq=gq, gqs=gqs, gk=gk, gks=gks,
        bias=_na_bias_table(na_rpb[l], n_rows),
    )


def _tiles(S):
    tm = min(512, S)
    return dict(tm=tm, tk=min(512, tm), mla_unroll=S // min(512, tm), post_tm=min(1024, S), ffn_tm=min(1024, S),
                ffn_tn=1408)


def kernel(x, mem, mix_norm_g, w_in, na_q_g, na_k_g, na_rpb, q_lat_g, kv_lat_g, w_uq, w_ukv, mla_q_g, mla_k_g,
           grp_out_g, w_out, mem_norm_g, mem_tok_norm_g, mem_w_q, mem_w_kv, mem_q_g, mem_k_g, mem_w_o,
           ffn_norm_g, ffn_w_up, ffn_conv_w, ffn_conv_b, ffn_w_down):
    B, S, D = x.shape
    depth = w_in.shape[0]
    t = _tiles(S)
    assert D == D_MODEL and S >= 2 * NA_TQ and t["tm"] == NA_TQ and S % t["tm"] == 0 and S % t["ffn_tm"] == 0
    cos, sin = _rope_tables_padded(S)
    row = lambda a: a[None, :]
    for l in range(depth):
        p = _layer_params(l, S // GRID_W, w_in, na_q_g, na_k_g, na_rpb, q_lat_g, kv_lat_g, w_uq, w_ukv, mla_q_g, mla_k_g)
        qnT, qnb, kn, vnT, qT, k, vT, qbound = _mixer_in(
            x, row(mix_norm_g[l]), p["wna"], p["wlat"], p["wuq"], p["wuqs"], p["wuk"], p["wuv"],
            p["gnaq"], p["gnak"], p["gql"], p["gkvl"], p["gq"], p["gqs"], p["gk"], p["gks"], cos, sin,
            tm=t["tm"], tk=t["tk"])
        tb = qbound[..., 0]
        rpb_range = (jnp.max(na_rpb[l], axis=(1, 2)) - jnp.min(na_rpb[l], axis=(1, 2))) * LOG2E
        na_guard = (2.0 * tb[:, :, MLA_HEADS:] + rpb_range).reshape(B, -1, NA_HEADS // NA_HPS, NA_HPS).max(axis=-1)
        out_aT = _na_attention(na_guard.transpose(0, 2, 1), qnT, qnb, kn, vnT, p["bias"])
        out_bT = _mla_attention(tb[:, :, :MLA_HEADS].transpose(0, 2, 1), qT, k, vT, unroll=t["mla_unroll"])
        km, vm = _mem_kv(mem, row(mem_tok_norm_g[l]), mem_w_kv[l].astype(BF16), row(mem_k_g[l]))
        x = _post_mixer(
            x, out_aT, out_bT, row(grp_out_g[l, :NA_WIDTH]), row(grp_out_g[l, NA_WIDTH:]), w_out[l].astype(BF16),
            row(mem_norm_g[l]), mem_w_q[l].astype(BF16), row(mem_q_g[l]), km, vm, mem_w_o[l].astype(BF16),
            tm=t["post_tm"])
        x = _conv_ffn(
            x.reshape(B * S, D), row(ffn_norm_g[l]), ffn_w_up[l].astype(BF16), ffn_conv_w[l], row(ffn_conv_b[l]),
            ffn_w_down[l].astype(BF16), tm=t["ffn_tm"], tn=t["ffn_tn"], seq=S).reshape(B, S, D)
    return x
```

```python
import functools

import jax
import jax.numpy as jnp
from jax import lax
from jax.experimental import pallas as pl
from jax.experimental.pallas import tpu as pltpu

F32 = jnp.float32
BF16 = jnp.bfloat16

D_MODEL = 1024
GRID_W = 64
EPS = 1e-6
NA_HEADS = 8
NA_HEAD_DIM = 64
NA_WIN_H = 8
NA_WIN_W = 16
NA_WIDTH = NA_HEADS * NA_HEAD_DIM
MLA_HEADS = 8
MLA_Q_RANK = 384
MLA_KV_RANK = 256
MLA_NOPE = 64
MLA_ROPE = 32
MLA_V = 64
MLA_QK = MLA_NOPE + MLA_ROPE
MLA_WIDTH = MLA_HEADS * MLA_V
ROPE_BASE = 10000.0
MEM_HEADS = 4
MEM_HEAD_DIM = 128
MEM_WIDTH = MEM_HEADS * MEM_HEAD_DIM
D_FF = 2816
LANES = 128
NA_TQ = NA_WIN_H * GRID_W
NA_CK = 4 * GRID_W
NA_HPS = 8
NEG = -1e30
LOG2E = 1.4426950408889634
MLA_AUG = MLA_QK
SAFE_BOUND = 50.0

VMEM_LIMIT = 48 * 1024 * 1024


def _params(sem):
    return pltpu.CompilerParams(dimension_semantics=sem, vmem_limit_bytes=VMEM_LIMIT)


def _rms(x, g):
    return x * lax.rsqrt(jnp.mean(x * x, axis=-1, keepdims=True) + EPS) * g


def _dot(a, b):
    return jnp.dot(a, b, preferred_element_type=F32)


def _dot_nt(a, b):
    return lax.dot_general(a, b, (((1,), (1,)), ((), ())), preferred_element_type=F32)


def _mixer_in_kernel(x_ref, gmix_ref, wna_ref, wlat_ref, wuq_ref, wuqs_ref, wuk_ref, wuv_ref,
                     gnaq_ref, gnak_ref, gql_ref, gkvl_ref, gq_ref, gqs_ref, gk_ref, gks_ref,
                     cos_ref, sin_ref,
                     qnT_ref, qnb_ref, kn_ref, vnT_ref, qT_ref, k_ref, vT_ref, qb_ref, *, tk):
    x = x_ref[0]
    tm = x.shape[0]
    hn = _rms(x, gmix_ref[...]).astype(BF16)

    zna = _dot(hn, wna_ref[...])
    lo = lax.broadcasted_iota(jnp.int32, (tm, LANES), 1) < NA_HEAD_DIM

    def seg_norm(z, g_ref, gscale):
        blocks = []
        for c in range(NA_WIDTH // LANES):
            sl = slice(c * LANES, (c + 1) * LANES)
            blk = z[:, sl]
            sq = blk * blk
            s_lo = jnp.sum(jnp.where(lo, sq, 0.0), axis=-1, keepdims=True)
            s_hi = jnp.sum(jnp.where(lo, 0.0, sq), axis=-1, keepdims=True)
            ms = jnp.where(lo, s_lo, s_hi) * (1.0 / NA_HEAD_DIM)
            blocks.append(blk * lax.rsqrt(ms + EPS) * (g_ref[:, sl] * gscale))
        return blocks

    qn = seg_norm(zna[:, :NA_WIDTH], gnaq_ref, NA_HEAD_DIM ** -0.5 * LOG2E)
    kn = seg_norm(zna[:, NA_WIDTH:2 * NA_WIDTH], gnak_ref, 1.0)
    kmax_na = (NA_HEAD_DIM ** 0.5) * jnp.max(jnp.abs(gnak_ref[...]), axis=-1, keepdims=True)
    na_tile_bounds = []
    for c in range(NA_WIDTH // LANES):
        for sub, qm in enumerate((jnp.where(lo, qn[c], 0.0), jnp.where(lo, 0.0, qn[c]))):
            qmT = qm.T
            bound = jnp.sqrt(jnp.sum(qmT * qmT, axis=0, keepdims=True)) * kmax_na
            qnT_ref[0, 2 * c + sub] = qmT.astype(BF16)
            qnb_ref[0, 2 * c + sub] = jnp.broadcast_to(bound, (8, tm))
            na_tile_bounds.append(jnp.broadcast_to(jnp.max(bound, axis=-1, keepdims=True), (1, LANES)))
        kn_ref[0, c] = kn[c].astype(BF16)
    vnT = zna[:, 2 * NA_WIDTH:].T.astype(BF16)
    for h in range(NA_HEADS):
        for c in range(tm // NA_CK):
            vnT_ref[0, h, c] = vnT[h * NA_HEAD_DIM:(h + 1) * NA_HEAD_DIM, c * NA_CK:(c + 1) * NA_CK]

    zlat = _dot(hn, wlat_ref[...])
    o1 = MLA_Q_RANK
    o2 = o1 + MLA_KV_RANK
    cqn = _rms(zlat[:, :o1], gql_ref[...]).astype(BF16)
    ckvn = _rms(zlat[:, o1:o2], gkvl_ref[...]).astype(BF16)
    kr = zlat[:, o2:o2 + LANES]
    krs = zlat[:, o2 + LANES:o2 + 2 * LANES]
    qpre = _dot(cqn, wuq_ref[...])
    qsw = _dot(cqn, wuqs_ref[...])
    kpre = _dot(ckvn, wuk_ref[...])
    v = _dot(ckvn, wuv_ref[...])
    cosv = cos_ref[...]
    sinv = sin_ref[...]
    gq = gq_ref[...]
    gqs = gqs_ref[...]
    gk = gk_ref[...]
    gks = gks_ref[...]
    qscale = MLA_QK ** -0.5 * LOG2E
    aug = lax.broadcasted_iota(jnp.int32, (tm, LANES), 1) == MLA_AUG
    kmax = (MLA_QK ** 0.5) * jnp.max(jnp.abs(gk), axis=-1, keepdims=True)
    tile_bounds = []
    for h in range(MLA_HEADS):
        sl = slice(h * LANES, (h + 1) * LANES)
        qb = qpre[:, sl]
        rinv = lax.rsqrt(jnp.sum(qb * qb, axis=-1, keepdims=True) * (1.0 / MLA_QK) + EPS)
        qr = ((qb * gq) * cosv + (qsw[:, sl] * gqs) * sinv) * (rinv * qscale)
        bound = jnp.sqrt(jnp.sum(qr * qr, axis=-1, keepdims=True)) * kmax
        qT_ref[0, h] = jnp.where(aug, -bound, qr).T.astype(BF16)
        tile_bounds.append(jnp.broadcast_to(jnp.max(bound, axis=0, keepdims=True), (1, LANES)))
        kb = kpre[:, sl] + kr
        rinv = lax.rsqrt(jnp.sum(kb * kb, axis=-1, keepdims=True) * (1.0 / MLA_QK) + EPS)
        kk = ((kb * gk) * cosv + (krs * gks) * sinv) * rinv
        k_ref[0, h] = jnp.where(aug, 1.0, kk).astype(BF16)
    qb_ref[0, 0] = jnp.concatenate(tile_bounds + na_tile_bounds, axis=0)
    vT = v.T.astype(BF16)
    for h in range(MLA_HEADS):
        for c in range(tm // tk):
            vT_ref[0, h, c] = vT[h * MLA_V:(h + 1) * MLA_V, c * tk:(c + 1) * tk]


def _mixer_in(x, gmix, wna, wlat, wuq, wuqs, wuk, wuv, gnaq, gnak, gql, gkvl, gq, gqs, gk, gks, cos, sin, *, tm, tk):
    B, S, D = x.shape
    nt = S // tm
    full = lambda a: pl.BlockSpec(a.shape, lambda b, i: (0,) * a.ndim)
    consts = (gmix, wna, wlat, wuq, wuqs, wuk, wuv, gnaq, gnak, gql, gkvl, gq, gqs, gk, gks)
    tok = lambda w: pl.BlockSpec((1, tm, w), lambda b, i: (b, i, 0))
    out_shape = (
        jax.ShapeDtypeStruct((B, NA_HEADS, LANES, S), BF16),
        jax.ShapeDtypeStruct((B, NA_HEADS, 8, S), F32),
        jax.ShapeDtypeStruct((B, NA_WIDTH // LANES, S, LANES), BF16),
        jax.ShapeDtypeStruct((B, NA_HEADS, S // NA_CK, NA_HEAD_DIM, NA_CK), BF16),
        jax.ShapeDtypeStruct((B, MLA_HEADS, LANES, S), BF16),
        jax.ShapeDtypeStruct((B, MLA_HEADS, S, LANES), BF16),
        jax.ShapeDtypeStruct((B, MLA_HEADS, S // tk, MLA_V, tk), BF16),
        jax.ShapeDtypeStruct((B, nt, MLA_HEADS + NA_HEADS, LANES), F32),
    )
    out_specs = (
        pl.BlockSpec((1, NA_HEADS, LANES, tm), lambda b, i: (b, 0, 0, i)),
        pl.BlockSpec((1, NA_HEADS, 8, tm), lambda b, i: (b, 0, 0, i)),
        pl.BlockSpec((1, NA_WIDTH // LANES, tm, LANES), lambda b, i: (b, 0, i, 0)),
        pl.BlockSpec((1, NA_HEADS, tm // NA_CK, NA_HEAD_DIM, NA_CK), lambda b, i: (b, 0, i, 0, 0)),
        pl.BlockSpec((1, MLA_HEADS, LANES, tm), lambda b, i: (b, 0, 0, i)),
        pl.BlockSpec((1, MLA_HEADS, tm, LANES), lambda b, i: (b, 0, i, 0)),
        pl.BlockSpec((1, MLA_HEADS, tm // tk, MLA_V, tk), lambda b, i: (b, 0, i, 0, 0)),
        pl.BlockSpec((1, 1, MLA_HEADS + NA_HEADS, LANES), lambda b, i: (b, i, 0, 0)),
    )
    return pl.pallas_call(
        functools.partial(_mixer_in_kernel, tk=tk),
        grid=(B, nt),
        in_specs=[tok(D)] + [full(a) for a in consts]
        + [pl.BlockSpec((tm, LANES), lambda b, i: (i, 0))] * 2,
        out_specs=out_specs,
        out_shape=out_shape,
        compiler_params=_params(("parallel", "parallel")),
        name="mixer_in",
    )(x, *consts, cos, sin)


def _na_kernel(guard_ref, qT_ref, qb_ref, k0_ref, k1_ref, k2_ref, k3_ref, v0_ref, v1_ref, v2_ref, v3_ref, t_ref,
               o_ref):
    tq = qT_ref.shape[3]
    k_refs = (k0_ref, k1_ref, k2_ref, k3_ref)
    v_refs = (v0_ref, v1_ref, v2_ref, v3_ref)
    tile = (pl.program_id(0) * pl.num_programs(1) + pl.program_id(1)) * pl.num_programs(2) + pl.program_id(2)
    safe = guard_ref[tile] <= 2.0 * SAFE_BOUND

    def finish(hh, l8, acc):
        o_ref[0, hh * NA_HEAD_DIM:(hh + 1) * NA_HEAD_DIM, :] = acc / jnp.sum(l8, axis=0, keepdims=True)

    def accumulate(p, v, l8, acc):
        return l8 + jnp.sum(p.reshape(NA_CK // 8, 8, tq), axis=0), acc + _dot(v, p.astype(BF16))

    @pl.when(safe)
    def _():
        for hh in range(NA_HPS):
            qT = qT_ref[0, hh]
            shift = qb_ref[0, hh, 0:1, :]
            nkeys = len(k_refs) * NA_CK
            kcat = jnp.concatenate([k_ref[0, hh // 2] for k_ref in k_refs], axis=0)
            vcat = jnp.concatenate([v_ref[0, hh, 0] for v_ref in v_refs], axis=1)
            p = jnp.exp2(_dot(kcat, qT) + t_ref[0, hh].reshape(nkeys, tq) - shift)
            finish(hh, jnp.sum(p.reshape(nkeys // 8, 8, tq), axis=0), _dot(vcat, p.astype(BF16)))

    @pl.when(jnp.logical_not(safe))
    def _():
        for hh in range(NA_HPS):
            qT = qT_ref[0, hh]
            scores = [_dot(k_ref[0, hh // 2], qT) + t_ref[0, hh, j] for j, k_ref in enumerate(k_refs)]
            m = jnp.max(jnp.maximum(jnp.maximum(scores[0], scores[1]), jnp.maximum(scores[2], scores[3])),
                        axis=0, keepdims=True)
            l8 = jnp.zeros((8, tq), F32)
            acc = jnp.zeros((NA_HEAD_DIM, tq), F32)
            for j, sc in enumerate(scores):
                l8, acc = accumulate(jnp.exp2(sc - m), v_refs[j][0, hh, 0], l8, acc)
            finish(hh, l8, acc)


def _na_attention(guard, qT, qb, k, vT, table):
    B, H, _, S = qT.shape
    G = S // NA_TQ
    nck = S // NA_CK
    per = NA_TQ // NA_CK

    def chunk(g, j):
        return jnp.clip(g * per - 1 + j, 0, nck - 1)

    kspec = lambda j: pl.BlockSpec((1, NA_HPS // 2, NA_CK, LANES), lambda b, hp, g: (b, hp, chunk(g, j), 0))
    vspec = lambda j: pl.BlockSpec((1, NA_HPS, 1, NA_HEAD_DIM, NA_CK), lambda b, hp, g: (b, hp, chunk(g, j), 0, 0))
    variant = lambda g: jnp.where(g == 0, 0, jnp.where(g == G - 1, 2, 1))
    return pl.pallas_call(
        _na_kernel,
        grid=(B, H // NA_HPS, G),
        in_specs=[pl.BlockSpec(memory_space=pltpu.SMEM),
                  pl.BlockSpec((1, NA_HPS, LANES, NA_TQ), lambda b, hp, g: (b, hp, 0, g)),
                  pl.BlockSpec((1, NA_HPS, 8, NA_TQ), lambda b, hp, g: (b, hp, 0, g))]
        + [kspec(j) for j in range(4)] + [vspec(j) for j in range(4)]
        + [pl.BlockSpec((1, NA_HPS) + table.shape[2:], lambda b, hp, g: (variant(g), hp, 0, 0, 0))],
        out_specs=pl.BlockSpec((1, NA_HPS * NA_HEAD_DIM, NA_TQ), lambda b, hp, g: (b, hp, g)),
        out_shape=jax.ShapeDtypeStruct((B, H * NA_HEAD_DIM, S), F32),
        compiler_params=_params(("parallel", "parallel", "parallel")),
        name="na_attn",
    )(guard.reshape(-1), qT, qb, k, k, k, k, vT, vT, vT, vT, table)


def _na_table_kernel(b2_ref, o_ref, *, n_rows):
    G = n_rows // NA_WIN_H
    masked = b2_ref.shape[1] - 1

    def block(g, j, kk, e):
        kr = NA_WIN_H * g - 4 + 4 * j + kk
        qr = NA_WIN_H * g + e
        start = min(max(qr - NA_WIN_H // 2, 0), n_rows - NA_WIN_H)
        if 0 <= kr < n_rows and start <= kr < start + NA_WIN_H:
            return kr - qr + (NA_WIN_H - 1)
        return masked

    for v, g in enumerate((0, 1, G - 1)):
        for j in range(4):
            for kk in range(4):
                o_ref[v, 0, j, kk * GRID_W:(kk + 1) * GRID_W, :] = jnp.concatenate(
                    [b2_ref[0, block(g, j, kk, e)] for e in range(NA_WIN_H)], axis=-1)


def _na_bias_table(rpb, n_rows):
    H = rpb.shape[0]
    col = jnp.arange(GRID_W)
    c0 = jnp.clip(col - NA_WIN_W // 2, 0, GRID_W - NA_WIN_W)
    inwin = (col[:, None] >= c0[None, :]) & (col[:, None] < c0[None, :] + NA_WIN_W)
    dc = jnp.clip(col[:, None] - col[None, :] + (NA_WIN_W - 1), 0, 2 * NA_WIN_W - 2)
    shifted = (rpb - jnp.max(rpb, axis=(1, 2), keepdims=True)) * LOG2E
    onehot = (dc[None] == jnp.arange(2 * NA_WIN_W - 1)[:, None, None]).astype(F32)
    spread = jnp.einsum("hdk,kcq->hdcq", shifted, onehot, precision=lax.Precision.HIGHEST)
    b2 = jnp.where(inwin[None, None], spread, NEG)
    b2 = jnp.concatenate([b2, jnp.full_like(b2[:, :1], NEG)], axis=1).astype(F32)
    return pl.pallas_call(
        functools.partial(_na_table_kernel, n_rows=n_rows),
        grid=(H,),
        in_specs=[pl.BlockSpec((1,) + b2.shape[1:], lambda h: (h, 0, 0, 0))],
        out_specs=pl.BlockSpec((3, 1, 4, NA_CK, NA_TQ), lambda h: (0, h, 0, 0, 0)),
        out_shape=jax.ShapeDtypeStruct((3, H, 4, NA_CK, NA_TQ), F32),
        compiler_params=_params(("parallel",)),
        name="na_table",
    )(b2)


def _mla_kernel(bound_ref, qT_ref, k_ref, vT_ref, o_ref, *, tk, unroll):
    qa = qT_ref[0, 0]
    tq = qa.shape[1]
    nk = k_ref.shape[2] // tk
    tile = (pl.program_id(0) * pl.num_programs(1) + pl.program_id(1)) * pl.num_programs(2) + pl.program_id(2)
    safe = bound_ref[tile] <= SAFE_BOUND

    def scores(c):
        off = pl.multiple_of(c * tk, tk)
        return _dot(k_ref[0, 0, pl.ds(off, tk), :], qa)

    @pl.when(safe)
    def _():
        def body(t, carry):
            l8, acc = carry
            c0 = t * unroll
            s = scores(c0)
            for u in range(unroll):
                s_next = scores(c0 + u + 1) if u + 1 < unroll else None
                p = jnp.exp2(s)
                l8 = l8 + jnp.sum(p.reshape(tk // 8, 8, tq), axis=0)
                acc = acc + _dot(vT_ref[0, 0, c0 + u], p.astype(BF16))
                s = s_next
            return l8, acc

        init = (jnp.zeros((8, tq), F32), jnp.zeros((MLA_V, tq), F32))
        l8, acc = lax.fori_loop(0, nk // unroll, body, init)
        o_ref[0] = acc / jnp.sum(l8, axis=0, keepdims=True)

    @pl.when(jnp.logical_not(safe))
    def _():
        def body(c, carry):
            m, l, acc = carry
            s = scores(c)
            m_new = jnp.maximum(m, jnp.max(s, axis=0, keepdims=True))
            alpha = jnp.exp2(m - m_new)
            p = jnp.exp2(s - m_new)
            l = alpha * l + jnp.sum(p, axis=0, keepdims=True)
            acc = alpha * acc + _dot(vT_ref[0, 0, c], p.astype(BF16))
            return m_new, l, acc

        init = (jnp.full((1, tq), NEG, F32), jnp.zeros((1, tq), F32), jnp.zeros((MLA_V, tq), F32))
        m, l, acc = lax.fori_loop(0, nk, body, init)
        o_ref[0] = acc / l


def _mla_attention(tile_bound, qT, k, vT, *, unroll):
    B, H, _, S = qT.shape
    tq = S // tile_bound.shape[2]
    tk = vT.shape[-1]
    return pl.pallas_call(
        functools.partial(_mla_kernel, tk=tk, unroll=unroll),
        grid=(B, H, S // tq),
        in_specs=[
            pl.BlockSpec(memory_space=pltpu.SMEM),
            pl.BlockSpec((1, 1, LANES, tq), lambda b, h, i: (b, h, 0, i)),
            pl.BlockSpec((1, 1, S, LANES), lambda b, h, i: (b, h, 0, 0)),
            pl.BlockSpec((1, 1, S // tk, MLA_V, tk), lambda b, h, i: (b, h, 0, 0, 0)),
        ],
        out_specs=pl.BlockSpec((1, MLA_V, tq), lambda b, h, i: (b, h, i)),
        out_shape=jax.ShapeDtypeStruct((B, H * MLA_V, S), F32),
        compiler_params=_params(("parallel", "parallel", "parallel")),
        name="mla_attn",
    )(tile_bound.reshape(-1), qT, k, vT)


def _mem_kv_kernel(mem_ref, g_ref, wkv_ref, gk_ref, k_ref, v_ref):
    mn = _rms(mem_ref[0], g_ref[...]).astype(BF16)
    kv = _dot(mn, wkv_ref[...])
    gk = gk_ref[...]
    for h in range(MEM_HEADS):
        sl = slice(h * LANES, (h + 1) * LANES)
        k_ref[0, :, sl] = _rms(kv[:, sl], gk).astype(BF16)
    v_ref[0] = kv[:, MEM_WIDTH:].astype(BF16)


def _mem_kv(mem, g, wkv, gk):
    B, M, D = mem.shape
    full = lambda a: pl.BlockSpec(a.shape, lambda b: (0,) * a.ndim)
    out = jax.ShapeDtypeStruct((B, M, MEM_WIDTH), BF16)
    ospec = pl.BlockSpec((1, M, MEM_WIDTH), lambda b: (b, 0, 0))
    return pl.pallas_call(
        _mem_kv_kernel,
        grid=(B,),
        in_specs=[pl.BlockSpec((1, M, D), lambda b: (b, 0, 0)), full(g), full(wkv), full(gk)],
        out_specs=(ospec, ospec),
        out_shape=(out, out),
        compiler_params=_params(("parallel",)),
        name="mem_kv",
    )(mem, g, wkv, gk)


def _post_mixer_kernel(x_ref, aT_ref, bT_ref, ga_ref, gb_ref, wout_ref, gmem_ref, wq_ref, gmq_ref,
                       km_ref, vm_ref, wo_ref, o_ref):
    x = x_ref[0]
    an = _rms(aT_ref[0].T, ga_ref[...]).astype(BF16)
    bn = _rms(bT_ref[0].T, gb_ref[...]).astype(BF16)
    x1 = x + _dot(an, wout_ref[:NA_WIDTH, :]) + _dot(bn, wout_ref[NA_WIDTH:, :])

    hq = _rms(x1, gmem_ref[...]).astype(BF16)
    q = _dot(hq, wq_ref[...])
    gq = gmq_ref[...] * (MEM_HEAD_DIM ** -0.5)
    outs = []
    for h in range(MEM_HEADS):
        sl = slice(h * LANES, (h + 1) * LANES)
        qn = _rms(q[:, sl], gq).astype(BF16)
        s = _dot_nt(qn, km_ref[0, :, sl])
        m = jnp.max(s, axis=-1, keepdims=True)
        p = jnp.exp(s - m)
        l = jnp.sum(p, axis=-1, keepdims=True)
        outs.append((_dot(p.astype(BF16), vm_ref[0, :, sl]) / l).astype(BF16))
    o_ref[0] = x1 + _dot(jnp.concatenate(outs, axis=-1), wo_ref[...])


def _post_mixer(x, aT, bT, ga, gb, wout, gmem, wq, gmq, km, vm, wo, *, tm):
    B, S, D = x.shape
    M = km.shape[1]
    full = lambda arr: pl.BlockSpec(arr.shape, lambda b, i: (0,) * arr.ndim)
    tok = lambda w: pl.BlockSpec((1, tm, w), lambda b, i: (b, i, 0))
    memspec = pl.BlockSpec((1, M, MEM_WIDTH), lambda b, i: (b, 0, 0))
    return pl.pallas_call(
        _post_mixer_kernel,
        grid=(B, S // tm),
        in_specs=[tok(D), pl.BlockSpec((1, NA_WIDTH, tm), lambda b, i: (b, 0, i)),
                  pl.BlockSpec((1, MLA_WIDTH, tm), lambda b, i: (b, 0, i)),
                  full(ga), full(gb), full(wout), full(gmem), full(wq), full(gmq), memspec, memspec, full(wo)],
        out_specs=tok(D),
        out_shape=jax.ShapeDtypeStruct((B, S, D), F32),
        compiler_params=_params(("parallel", "parallel")),
        name="post_mixer",
    )(x, aT, bT, ga, gb, wout, gmem, wq, gmq, km, vm, wo)


HALO = 8


def _ffn_kernel(x_ref, xp_ref, xn_ref, g_ref, wg_ref, wv_ref, cwg_ref, cwv_ref, cbg_ref, cbv_ref, wd_ref,
                o_ref, hn_ref, acc_ref, *, tiles_per_seq):
    i = pl.program_id(0)
    j = pl.program_id(1)
    tm = x_ref.shape[0]
    tn = wg_ref.shape[1]

    @pl.when(j == 0)
    def _():
        g = g_ref[...]
        hn_ref[:tm, :] = _rms(x_ref[...], g).astype(BF16)
        hn_ref[tm:tm + HALO, :] = _rms(xp_ref[...], g).astype(BF16)
        hn_ref[tm + HALO:, :] = _rms(xn_ref[...], g).astype(BF16)
        acc_ref[...] = jnp.zeros_like(acc_ref)

    hn = hn_ref[...]
    has_prev = (i % tiles_per_seq != 0).astype(F32)
    has_next = ((i + 1) % tiles_per_seq != 0).astype(F32)
    row = lax.broadcasted_iota(jnp.int32, (tm, tn), 0)

    def conv(w_ref, cw_ref, cb_ref):
        ua = _dot(hn, w_ref[...])
        u = ua[:tm]
        prev = ua[tm + HALO - 1:tm + HALO, :] * has_prev
        nxt = ua[tm + HALO:tm + HALO + 1, :] * has_next
        um = jnp.where(row == 0, prev, pltpu.roll(u, 1, 0))
        up = jnp.where(row == tm - 1, nxt, pltpu.roll(u, tm - 1, 0))
        return um * cw_ref[0:1, :] + u * cw_ref[1:2, :] + up * cw_ref[2:3, :] + cb_ref[...]

    gate = conv(wg_ref, cwg_ref, cbg_ref)
    val = conv(wv_ref, cwv_ref, cbv_ref)
    act = gate / (1.0 + jnp.exp(-gate)) * val
    acc_ref[...] += _dot(act.astype(BF16), wd_ref[...])

    @pl.when(j == pl.num_programs(1) - 1)
    def _():
        o_ref[...] = x_ref[...] + acc_ref[...]


def _conv_ffn(x, g, wup, cw, cb, wd, *, tm, tn, seq):
    T, D = x.shape
    nj = D_FF // tn
    nhalo = T // HALO
    per = tm // HALO
    return pl.pallas_call(
        functools.partial(_ffn_kernel, tiles_per_seq=seq // tm),
        grid=(T // tm, nj),
        in_specs=[
            pl.BlockSpec((tm, D), lambda i, j: (i, 0)),
            pl.BlockSpec((HALO, D), lambda i, j: (jnp.maximum(i * per - 1, 0), 0)),
            pl.BlockSpec((HALO, D), lambda i, j: (jnp.minimum((i + 1) * per, nhalo - 1), 0)),
            pl.BlockSpec(g.shape, lambda i, j: (0, 0)),
            pl.BlockSpec((D, tn), lambda i, j: (0, j), pipeline_mode=pl.Buffered(1)),
            pl.BlockSpec((D, tn), lambda i, j: (0, nj + j), pipeline_mode=pl.Buffered(1)),
            pl.BlockSpec((3, tn), lambda i, j: (0, j)),
            pl.BlockSpec((3, tn), lambda i, j: (0, nj + j)),
            pl.BlockSpec((1, tn), lambda i, j: (0, j)),
            pl.BlockSpec((1, tn), lambda i, j: (0, nj + j)),
            pl.BlockSpec((tn, D), lambda i, j: (j, 0), pipeline_mode=pl.Buffered(1)),
        ],
        out_specs=pl.BlockSpec((tm, D), lambda i, j: (i, 0)),
        out_shape=jax.ShapeDtypeStruct((T, D), F32),
        scratch_shapes=[pltpu.VMEM((tm + 2 * HALO, D), BF16), pltpu.VMEM((tm, D), F32)],
        compiler_params=_params(("parallel", "arbitrary")),
        name="conv_ffn",
    )(x, x, x, g, wup, wup, cw, cw, cb, cb, wd)


def _head_pad(w, heads, width):
    k = w.shape[0]
    w = w.reshape(k, heads, width)
    return jnp.pad(w, ((0, 0), (0, 0), (0, LANES - width))).reshape(k, heads * LANES)


def _rope_swap(a):
    half = a.shape[-1] // 2
    return jnp.concatenate([a[..., half:], a[..., :half]], axis=-1)


def _rope_sign():
    half = MLA_ROPE // 2
    return jnp.concatenate([-jnp.ones((half,), F32), jnp.ones((half,), F32)])


def _rope_tables_padded(seq):
    t = jnp.arange(seq)
    row = (t // GRID_W).astype(F32)
    col = (t % GRID_W).astype(F32)
    nfreq = MLA_ROPE // 4
    inv = 1.0 / (ROPE_BASE ** (jnp.arange(nfreq, dtype=F32) / nfreq))
    ang = jnp.concatenate([row[:, None] * inv, col[:, None] * inv], axis=-1)
    cos = jnp.concatenate([jnp.cos(ang)] * 2, axis=-1)
    sin = jnp.concatenate([jnp.sin(ang)] * 2, axis=-1)
    pad = LANES - MLA_QK
    cos = jnp.concatenate([jnp.ones((seq, MLA_NOPE), F32), cos, jnp.zeros((seq, pad), F32)], axis=-1)
    sin = jnp.concatenate([jnp.zeros((seq, MLA_NOPE), F32), sin, jnp.zeros((seq, pad), F32)], axis=-1)
    return cos, sin


def _layer_params(l, n_rows, w_in, na_q_g, na_k_g, na_rpb, q_lat_g, kv_lat_g, w_uq, w_ukv, mla_q_g, mla_k_g):
    o1, o2, o3 = NA_WIDTH, 2 * NA_WIDTH, 3 * NA_WIDTH
    o4 = o3 + MLA_Q_RANK
    o5 = o4 + MLA_KV_RANK
    win = w_in[l]
    sign = _rope_sign()
    w_kr = win[:, o5:]
    pad_rope = lambda w: jnp.pad(w, ((0, 0), (MLA_NOPE, LANES - MLA_QK)))
    wlat = jnp.concatenate([win[:, o3:o5], pad_rope(w_kr), pad_rope(_rope_swap(w_kr) * sign)], axis=-1)

    wq3 = w_uq[l].reshape(MLA_Q_RANK, MLA_HEADS, MLA_QK)
    wq3s = jnp.concatenate([jnp.zeros_like(wq3[..., :MLA_NOPE]), _rope_swap(wq3[..., MLA_NOPE:]) * sign], axis=-1)
    wkv3 = w_ukv[l].reshape(MLA_KV_RANK, MLA_HEADS, MLA_NOPE + MLA_V)

    def gain_pair(g):
        gs = jnp.concatenate([jnp.zeros((MLA_NOPE,), F32), _rope_swap(g[MLA_NOPE:])])
        padl = lambda a: jnp.pad(a, (0, LANES - MLA_QK))[None, :]
        return padl(g), padl(gs)

    gq, gqs = gain_pair(mla_q_g[l])
    gk, gks = gain_pair(mla_k_g[l])
    return dict(
        wna=win[:, :o3].astype(BF16),
        wlat=wlat.astype(BF16),
        wuq=_head_pad(wq3.reshape(MLA_Q_RANK, -1), MLA_HEADS, MLA_QK).astype(BF16),
        wuqs=_head_pad(wq3s.reshape(MLA_Q_RANK, -1), MLA_HEADS, MLA_QK).astype(BF16),
        wuk=_head_pad(wkv3[..., :MLA_NOPE].reshape(MLA_KV_RANK, -1), MLA_HEADS, MLA_NOPE).astype(BF16),
        wuv=wkv3[..., MLA_NOPE:].reshape(MLA_KV_RANK, -1).astype(BF16),
        gnaq=jnp.tile(na_q_g[l], NA_HEADS)[None, :],
        gnak=jnp.tile(na_k_g[l], NA_HEADS)[None, :],
        gql=q_lat_g[l][None, :],
        gkvl=kv_lat_g[l][None, :],
        gq=gq, gqs=gqs, gk=gk, gks=gks,
        bias=_na_bias_table(na_rpb[l], n_rows),
    )


def _tiles(S):
    tm = min(512, S)
    return dict(tm=tm, tk=min(512, tm), mla_unroll=S // min(512, tm), post_tm=min(1024, S), ffn_tm=min(512, S),
                ffn_tn=2816)


def kernel(x, mem, mix_norm_g, w_in, na_q_g, na_k_g, na_rpb, q_lat_g, kv_lat_g, w_uq, w_ukv, mla_q_g, mla_k_g,
           grp_out_g, w_out, mem_norm_g, mem_tok_norm_g, mem_w_q, mem_w_kv, mem_q_g, mem_k_g, mem_w_o,
           ffn_norm_g, ffn_w_up, ffn_conv_w, ffn_conv_b, ffn_w_down):
    B, S, D = x.shape
    depth = w_in.shape[0]
    t = _tiles(S)
    assert D == D_MODEL and S >= 2 * NA_TQ and t["tm"] == NA_TQ and S % t["tm"] == 0 and S % t["ffn_tm"] == 0
    cos, sin = _rope_tables_padded(S)
    row = lambda a: a[None, :]
    for l in range(depth):
        p = _layer_params(l, S // GRID_W, w_in, na_q_g, na_k_g, na_rpb, q_lat_g, kv_lat_g, w_uq, w_ukv, mla_q_g, mla_k_g)
        qnT, qnb, kn, vnT, qT, k, vT, qbound = _mixer_in(
            x, row(mix_norm_g[l]), p["wna"], p["wlat"], p["wuq"], p["wuqs"], p["wuk"], p["wuv"],
            p["gnaq"], p["gnak"], p["gql"], p["gkvl"], p["gq"], p["gqs"], p["gk"], p["gks"], cos, sin,
            tm=t["tm"], tk=t["tk"])
        tb = qbound[..., 0]
        rpb_range = (jnp.max(na_rpb[l], axis=(1, 2)) - jnp.min(na_rpb[l], axis=(1, 2))) * LOG2E
        na_guard = (2.0 * tb[:, :, MLA_HEADS:] + rpb_range).reshape(B, -1, NA_HEADS // NA_HPS, NA_HPS).max(axis=-1)
        out_aT = _na_attention(na_guard.transpose(0, 2, 1), qnT, qnb, kn, vnT, p["bias"])
        out_bT = _mla_attention(tb[:, :, :MLA_HEADS].transpose(0, 2, 1), qT, k, vT, unroll=t["mla_unroll"])
        km, vm = _mem_kv(mem, row(mem_tok_norm_g[l]), mem_w_kv[l].astype(BF16), row(mem_k_g[l]))
        x = _post_mixer(
            x, out_aT, out_bT, row(grp_out_g[l, :NA_WIDTH]), row(grp_out_g[l, NA_WIDTH:]), w_out[l].astype(BF16),
            row(mem_norm_g[l]), mem_w_q[l].astype(BF16), row(mem_q_g[l]), km, vm, mem_w_o[l].astype(BF16),
            tm=t["post_tm"])
        x = _conv_ffn(
            x.reshape(B * S, D), row(ffn_norm_g[l]), ffn_w_up[l].astype(BF16), ffn_conv_w[l], row(ffn_conv_b[l]),
            ffn_w_down[l].astype(BF16), tm=t["ffn_tm"], tn=t["ffn_tn"], seq=S).reshape(B, S, D)
    return x
```
